```python
import functools
import jax, jax.numpy as jnp
from jax import lax
import numpy as np

D_MODEL = 1024
BATCH = 1
SEQ = 16384
DEPTH = 1
DEC_BATCH = 128
DEC_SEQ = 8
PAST_LEN = 8192
PAGE_SIZE = 128

N_HEADS = 8
HEAD_DIM = 128
N_KV_HEADS = 2
GROUP = N_HEADS // N_KV_HEADS
D_ATTN = N_HEADS * HEAD_DIM
D_KV = N_KV_HEADS * HEAD_DIM
IDX_HEADS = 8
IDX_DIM = 64
TOPK_MAX = 256
D_LRU = 1024
LRU_BLOCKS = 8
LRU_BLOCK = D_LRU // LRU_BLOCKS
CONV_W = 4
LRU_C = 8.0
D_FF = 4 * D_MODEL
Q_BLOCK = 128
EPS = 1e-6
SPLITS = (D_ATTN, D_KV, D_KV, IDX_HEADS * IDX_DIM, IDX_HEADS, IDX_DIM, D_LRU, D_LRU, D_MODEL, D_MODEL)
D_IN = D_ATTN + 2 * D_KV + IDX_HEADS * IDX_DIM + IDX_HEADS + IDX_DIM + 2 * D_LRU + 2 * D_MODEL

kernel_name = 'hybrid_dsa_rglru_decode_step'


def rmsnorm(x, g):
    xf = x.astype(jnp.float32)
    y = xf * lax.rsqrt(jnp.mean(xf * xf, axis=-1, keepdims=True) + EPS)
    return (y * g.astype(jnp.float32)).astype(x.dtype)


def split_projection(p):
    parts, start = [], 0
    for n in SPLITS:
        parts.append(p[..., start:start + n])
        start += n
    return parts


_take = jax.vmap(lambda a, i: a[i])


def sparse_attend(q, iq, iw, ik_all, qpos, n_sel, gather_kv):
    B, T = q.shape[:2]
    L = ik_all.shape[1]
    logits = jnp.einsum('bthd,bsd->bths', iq, ik_all).astype(jnp.float32) * (IDX_DIM ** -0.5)
    score = jnp.einsum('bths,bth->bts', jax.nn.relu(logits), iw.astype(jnp.float32) * (IDX_HEADS ** -0.5))
    admissible = jnp.arange(L)[None, :] <= qpos[:, None]
    score = jnp.where(admissible[None], score, -jnp.inf)
    _, sel = lax.top_k(score, n_sel)
    valid = sel <= qpos[None, :, None]
    kg, vg = gather_kv(sel)
    qg = q.reshape(B, T, N_KV_HEADS, GROUP, HEAD_DIM)
    s = jnp.einsum('btgrd,btkgd->btgrk', qg, kg).astype(jnp.float32) * (HEAD_DIM ** -0.5)
    s = jnp.where(valid[:, :, None, None, :], s, -jnp.inf)
    p = jax.nn.softmax(s, axis=-1).astype(vg.dtype)
    o = jnp.einsum('btgrk,btkgd->btgrd', p, vg)
    return o.reshape(B, T, D_ATTN)


def prompt_attention(q, k, v, iq, iw, ik):
    B, S = q.shape[:2]
    n_blk = S // Q_BLOCK
    n_sel = min(TOPK_MAX, S // 4)

    def gather_kv(sel):
        return _take(k, sel), _take(v, sel)

    def blocks(t):
        return t.reshape(B, n_blk, Q_BLOCK, *t.shape[2:]).swapaxes(0, 1)

    def one_block(args):
        qb, iqb, iwb, start = args
        qpos = start + jnp.arange(Q_BLOCK)
        return sparse_attend(qb, iqb, iwb, ik, qpos, n_sel, gather_kv)

    out = lax.map(one_block, (blocks(q), blocks(iq), blocks(iw), jnp.arange(n_blk) * Q_BLOCK))
    return out.swapaxes(0, 1).reshape(B, S, D_ATTN)


def sample_attention(q, k, v, iq, iw, ik, cache_k, cache_v, cache_idx_k, page_table, layer):
    DB, T = q.shape[:2]
    n_sel = min(TOPK_MAX, (PAST_LEN + T) // 4)
    past_ik = cache_idx_k[layer, page_table].reshape(DB, PAST_LEN, IDX_DIM).astype(ik.dtype)
    ik_all = jnp.concatenate([past_ik, ik], axis=1)
    qpos = PAST_LEN + jnp.arange(T)

    def gather_kv(sel):
        is_past = (sel < PAST_LEN)[..., None, None]
        ps = jnp.minimum(sel, PAST_LEN - 1)
        phys = _take(page_table, ps // PAGE_SIZE)
        off = ps % PAGE_SIZE
        ns = jnp.clip(sel - PAST_LEN, 0, T - 1)
        kg = jnp.where(is_past, cache_k[layer, phys, off].astype(k.dtype), _take(k, ns))
        vg = jnp.where(is_past, cache_v[layer, phys, off].astype(v.dtype), _take(v, ns))
        return kg, vg

    return sparse_attend(q, iq, iw, ik_all, qpos, n_sel, gather_kv)


def rglru_branch(xl, xg, conv_buf, h0, w_conv, b_conv, w_rg, b_rg, w_ig, b_ig, lru_lambda):
    B, T = xl.shape[:2]
    xc = jnp.concatenate([conv_buf.astype(xl.dtype), xl], axis=1)
    u = b_conv
    for j in range(CONV_W):
        u = u + w_conv[j] * xc[:, j:j + T]
    new_buf = xc[:, T:]
    ub = u.reshape(B, T, LRU_BLOCKS, LRU_BLOCK)
    r = jax.nn.sigmoid((jnp.einsum('btnc,ncd->btnd', ub, w_rg).reshape(B, T, D_LRU) + b_rg).astype(jnp.float32))
    ig = jax.nn.sigmoid((jnp.einsum('btnc,ncd->btnd', ub, w_ig).reshape(B, T, D_LRU) + b_ig).astype(jnp.float32))
    log_a = -LRU_C * r * jax.nn.softplus(-lru_lambda.astype(jnp.float32))
    a = jnp.exp(log_a)
    bx = jnp.sqrt(-jnp.expm1(2.0 * log_a)) * (ig * u.astype(jnp.float32))
    bx = bx.at[:, 0].add(a[:, 0] * h0.astype(jnp.float32))

    def combine(lhs, rhs):
        return (lhs[0] * rhs[0], rhs[0] * lhs[1] + rhs[1])

    _, h = lax.associative_scan(combine, (a, bx), axis=1)
    out = (h * jax.nn.gelu(xg.astype(jnp.float32))).astype(xl.dtype)
    return out, h[:, -1].astype(xl.dtype), new_buf


def trunk_layer(x, attend, conv_buf, h0, w_in, w_conv, b_conv, w_rg, b_rg, w_ig, b_ig, lru_lambda,
                w_up_attn, w_up_lru, w_out, g_mix_pre, g_mix_post, g_ffn_pre, g_ffn_post, w_ff1, w_ff2):
    B, T, _ = x.shape
    hn = rmsnorm(x, g_mix_pre)
    q, k, v, iq, iw, ik, xl, xg, ga, gb = split_projection(hn @ w_in)
    q = q.reshape(B, T, N_HEADS, HEAD_DIM)
    k = k.reshape(B, T, N_KV_HEADS, HEAD_DIM)
    v = v.reshape(B, T, N_KV_HEADS, HEAD_DIM)
    iq = iq.reshape(B, T, IDX_HEADS, IDX_DIM)
    o_attn = attend(q, k, v, iq, iw, ik)
    o_lru, h_last, new_buf = rglru_branch(xl, xg, conv_buf, h0, w_conv, b_conv, w_rg, b_rg, w_ig, b_ig, lru_lambda)
    merged = jax.nn.sigmoid(ga) * (o_attn @ w_up_attn) + jax.nn.sigmoid(gb) * (o_lru @ w_up_lru)
    x = x + rmsnorm(merged @ w_out, g_mix_post)
    hf = rmsnorm(x, g_ffn_pre)
    f = jnp.square(jax.nn.relu(hf @ w_ff1)) @ w_ff2
    x = x + rmsnorm(f, g_ffn_post)
    return x, (k, v, ik, h_last, new_buf)


def setup_inputs(seed: int = 0) -> dict:
    key = jax.random.key(seed)
    ks = iter(jax.random.split(key, 32))

    def nrm(shape, scale):
        return jax.random.normal(next(ks), shape, jnp.float32) * scale

    n_pages = PAST_LEN // PAGE_SIZE
    n_used = DEC_BATCH * n_pages
    n_pool = n_used + max(1, n_used // 4)
    perm = jax.random.permutation(next(ks), n_pool)
    page_table = perm[:n_used].reshape(DEC_BATCH, n_pages).astype(jnp.int32)
    u = jax.random.uniform(next(ks), (DEPTH, D_LRU), jnp.float32, 0.9, 0.999)
    a_base = u ** (1.0 / LRU_C)
    lru_lambda = jnp.log(a_base) - jnp.log1p(-a_base)

    def gain():
        return 1.0 + nrm((DEPTH, D_MODEL), 0.01)

    return {
        'x_prompt': nrm((BATCH, SEQ, D_MODEL), 1.0),
        'x_sample': nrm((DEC_BATCH, DEC_SEQ, D_MODEL), 1.0),
        'cache_k': nrm((DEPTH, n_pool, PAGE_SIZE, N_KV_HEADS, HEAD_DIM), 1.0),
        'cache_v': nrm((DEPTH, n_pool, PAGE_SIZE, N_KV_HEADS, HEAD_DIM), 1.0),
        'cache_idx_k': nrm((DEPTH, n_pool, PAGE_SIZE, IDX_DIM), 1.0),
        'state_h': nrm((DEPTH, DEC_BATCH, D_LRU), 0.5),
        'state_conv': nrm((DEPTH, DEC_BATCH, CONV_W - 1, D_LRU), 1.0),
        'page_table': page_table,
        'w_in': nrm((DEPTH, D_MODEL, D_IN), D_MODEL ** -0.5),
        'w_conv': nrm((DEPTH, CONV_W, D_LRU), CONV_W ** -0.5),
        'b_conv': nrm((DEPTH, D_LRU), 0.01),
        'w_rg': nrm((DEPTH, LRU_BLOCKS, LRU_BLOCK, LRU_BLOCK), LRU_BLOCK ** -0.5),
        'b_rg': nrm((DEPTH, D_LRU), 0.01),
        'w_ig': nrm((DEPTH, LRU_BLOCKS, LRU_BLOCK, LRU_BLOCK), LRU_BLOCK ** -0.5),
        'b_ig': nrm((DEPTH, D_LRU), 0.01),
        'lru_lambda': lru_lambda,
        'w_up_attn': nrm((DEPTH, D_ATTN, D_MODEL), D_ATTN ** -0.5),
        'w_up_lru': nrm((DEPTH, D_LRU, D_MODEL), D_LRU ** -0.5),
        'w_out': nrm((DEPTH, D_MODEL, D_MODEL), D_MODEL ** -0.5),
        'g_mix_pre': gain(),
        'g_mix_post': gain(),
        'g_ffn_pre': gain(),
        'g_ffn_post': gain(),
        'w_ff1': nrm((DEPTH, D_MODEL, D_FF), D_MODEL ** -0.5),
        'w_ff2': nrm((DEPTH, D_FF, D_MODEL), D_FF ** -0.5),
    }


def reference(x_prompt, x_sample, cache_k, cache_v, cache_idx_k, state_h, state_conv, page_table,
              w_in, w_conv, b_conv, w_rg, b_rg, w_ig, b_ig, lru_lambda, w_up_attn, w_up_lru, w_out,
              g_mix_pre, g_mix_post, g_ffn_pre, g_ffn_post, w_ff1, w_ff2):
    yp, ys = x_prompt, x_sample
    new_p, new_s = [], []
    for l in range(DEPTH):
        weights = (w_in[l], w_conv[l], b_conv[l], w_rg[l], b_rg[l], w_ig[l], b_ig[l], lru_lambda[l],
                   w_up_attn[l], w_up_lru[l], w_out[l], g_mix_pre[l], g_mix_post[l], g_ffn_pre[l],
                   g_ffn_post[l], w_ff1[l], w_ff2[l])
        bp = x_prompt.shape[0]
        zero_buf = jnp.zeros((bp, CONV_W - 1, D_LRU), x_prompt.dtype)
        zero_h = jnp.zeros((bp, D_LRU), x_prompt.dtype)
        yp, st_p = trunk_layer(yp, prompt_attention, zero_buf, zero_h, *weights)
        attend_s = functools.partial(sample_attention, cache_k=cache_k, cache_v=cache_v,
                                     cache_idx_k=cache_idx_k, page_table=page_table, layer=l)
        ys, st_s = trunk_layer(ys, attend_s, state_conv[l], state_h[l], *weights)
        new_p.append(st_p)
        new_s.append(st_s)

    def stack(states, j):
        return jnp.stack([s[j] for s in states], axis=0)

    return (yp, ys,
            stack(new_p, 0), stack(new_p, 1), stack(new_p, 2), stack(new_p, 3), stack(new_p, 4),
            stack(new_s, 0), stack(new_s, 1), stack(new_s, 2), stack(new_s, 3), stack(new_s, 4))
```

```python
import functools
import math

import jax
import jax.numpy as jnp
from jax import lax
from jax.experimental import pallas as pl
from jax.experimental.pallas import tpu as pltpu

N_HEADS = 8
HEAD_DIM = 128
N_KV_HEADS = 2
GROUP = N_HEADS // N_KV_HEADS
IDX_HEADS = 8
IDX_DIM = 64
TOPK_MAX = 256
LRU_BLOCKS = 8
CONV_W = 4
LRU_C = 8.0
EPS = 1e-6

LANES = 128
SUBLANES = 8
VMEM_LIMIT = 56 * 1024 * 1024

F32 = jnp.float32
BF16 = jnp.bfloat16
NEG_BIG = -1e30
IDX_BIG = 1e9
LOG2E = math.log2(math.e)


def _const_spec(shape):
    nd = len(shape)
    return pl.BlockSpec(shape, lambda *_: (0,) * nd, pipeline_mode=pl.Buffered(1))


def _params(n_grid):
    return pltpu.CompilerParams(dimension_semantics=("arbitrary",) * n_grid,
                                vmem_limit_bytes=VMEM_LIMIT)


def _rmsnorm(x, g):
    return x * lax.rsqrt(jnp.mean(x * x, axis=-1, keepdims=True) + EPS) * g


def _dot(a, b):
    return jnp.dot(a, b, preferred_element_type=F32)


def _dot_nt(a, b):
    return lax.dot_general(a, b, (((1,), (1,)), ((), ())), preferred_element_type=F32)


def _inproj_body(x_ref, g_ref, wa_ref, wi_ref, wl_ref,
                 q_ref, k_ref, v_ref, kb_ref, vb_ref, iq_ref, ikw_ref, xl_ref, xg_ref, ga_ref, gb_ref,
                 *, d_attn, d_kv, d_lru, d_model):
    hn = _rmsnorm(x_ref[...], g_ref[...]).astype(BF16)
    q_ref[...] = _dot(hn, wa_ref[:, :d_attn]).astype(BF16)
    k = _dot(hn, wa_ref[:, d_attn:d_attn + d_kv])
    k_ref[...] = k
    kb_ref[...] = k.astype(BF16)
    v = _dot(hn, wa_ref[:, d_attn + d_kv:d_attn + 2 * d_kv])
    v_ref[...] = v
    vb_ref[...] = v.astype(BF16)
    iq = _dot(hn, wa_ref[:, d_attn + 2 * d_kv:])
    for h in range(IDX_HEADS):
        iq_ref[h] = iq[:, h * IDX_DIM:(h + 1) * IDX_DIM].astype(BF16)
    ikw_ref[...] = _dot(hn, wi_ref[...])
    xl_ref[...] = _dot(hn, wl_ref[:, :d_lru])
    xg_ref[...] = _dot(hn, wl_ref[:, d_lru:2 * d_lru])
    ga_ref[...] = _dot(hn, wl_ref[:, 2 * d_lru:2 * d_lru + d_model])
    gb_ref[...] = _dot(hn, wl_ref[:, 2 * d_lru + d_model:])


def _in_projection(x, g, wa, wi, wl, *, d_attn, d_kv, d_lru, tm):
    n, d_model = x.shape
    grid = (n // tm,)
    row = lambda w: pl.BlockSpec((tm, w), lambda i: (i, 0))
    out_shape = (
        jax.ShapeDtypeStruct((n, d_attn), BF16),
        jax.ShapeDtypeStruct((n, d_kv), F32),
        jax.ShapeDtypeStruct((n, d_kv), F32),
        jax.ShapeDtypeStruct((n, d_kv), BF16),
        jax.ShapeDtypeStruct((n, d_kv), BF16),
        jax.ShapeDtypeStruct((IDX_HEADS, n, IDX_DIM), BF16),
        jax.ShapeDtypeStruct((n, LANES), F32),
        jax.ShapeDtypeStruct((n, d_lru), F32),
        jax.ShapeDtypeStruct((n, d_lru), F32),
        jax.ShapeDtypeStruct((n, d_model), F32),
        jax.ShapeDtypeStruct((n, d_model), F32),
    )
    out_specs = (row(d_attn), row(d_kv), row(d_kv), row(d_kv), row(d_kv),
                 pl.BlockSpec((IDX_HEADS, tm, IDX_DIM), lambda i: (0, i, 0)),
                 row(LANES), row(d_lru), row(d_lru), row(d_model), row(d_model))
    body = functools.partial(_inproj_body, d_attn=d_attn, d_kv=d_kv, d_lru=d_lru, d_model=d_model)
    return pl.pallas_call(
        body, grid=grid,
        in_specs=[row(d_model), _const_spec(g.shape), _const_spec(wa.shape), _const_spec(wi.shape),
                  _const_spec(wl.shape)],
        out_specs=out_specs, out_shape=out_shape, compiler_params=_params(1), name="in_projection",
    )(x, g, wa, wi, wl)


def _shift_rows(cur, prev, j, row_in_seg):
    tm = cur.shape[0]
    from_prev = pltpu.roll(prev, tm - SUBLANES + j, axis=0)
    return jnp.where(row_in_seg < j, from_prev, pltpu.roll(cur, j, axis=0))


def _lru_core(xl, xg, prev, h_in, wc, bc, wrg, brg, wig, big, lam, *, seg):
    tm, d = xl.shape
    row = lax.broadcasted_iota(jnp.int32, (tm, 1), 0)
    row_in_seg = row % seg
    u = bc + wc[CONV_W - 1:CONV_W] * xl
    for j in range(1, CONV_W):
        u = u + wc[CONV_W - 1 - j:CONV_W - j] * _shift_rows(xl, prev, j, row_in_seg)
    ub = u.astype(BF16)
    blk = d // LRU_BLOCKS
    r = jnp.concatenate([_dot(ub[:, n * blk:(n + 1) * blk], wrg[n]) for n in range(LRU_BLOCKS)], axis=1)
    ig = jnp.concatenate([_dot(ub[:, n * blk:(n + 1) * blk], wig[n]) for n in range(LRU_BLOCKS)], axis=1)
    r = jax.nn.sigmoid(r + brg)
    ig = jax.nn.sigmoid(ig + big)
    softplus_neg = jnp.maximum(-lam, 0.0) + jnp.log1p(jnp.exp(-jnp.abs(lam)))
    log_a = -LRU_C * r * softplus_neg
    a = jnp.exp(log_a)
    b = jnp.sqrt(-jnp.tanh(log_a) * (a * a + 1.0)) * (ig * u)
    d_step = 1
    while d_step < seg:
        keep = row_in_seg >= d_step
        a_sh = jnp.where(keep, pltpu.roll(a, d_step, axis=0), 1.0)
        b_sh = jnp.where(keep, pltpu.roll(b, d_step, axis=0), 0.0)
        b = a * b_sh + b
        a = a * a_sh
        d_step *= 2
    h = a * h_in + b
    gelu = 0.5 * xg * (1.0 + jnp.tanh(math.sqrt(2.0 / math.pi) * (xg + 0.044715 * (xg * xg * xg))))
    return h * gelu, h


def _lru_prompt_body(xl_ref, xg_ref, wc_ref, bc_ref, wrg_ref, brg_ref, wig_ref, big_ref, lam_ref,
                     o_ref, h_ref, tail_ref, hc_sc, prev_sc):
    @pl.when(pl.program_id(0) == 0)
    def _():
        hc_sc[...] = jnp.zeros_like(hc_sc)
        prev_sc[...] = jnp.zeros_like(prev_sc)

    xl = xl_ref[...]
    tm = xl.shape[0]
    prev = jnp.tile(prev_sc[...], (tm // SUBLANES, 1))
    out, h = _lru_core(xl, xg_ref[...], prev, hc_sc[0:1, :], wc_ref[...], bc_ref[...], wrg_ref[...],
                       brg_ref[...], wig_ref[...], big_ref[...], lam_ref[...], seg=tm)
    o_ref[...] = out.astype(BF16)
    last = h[tm - SUBLANES:, :]
    hc_sc[...] = jnp.broadcast_to(last[SUBLANES - 1:SUBLANES, :], hc_sc.shape)
    h_ref[...] = last
    prev_sc[...] = xl[tm - SUBLANES:, :]
    tail_ref[...] = xl[tm - SUBLANES:, :]


def _lru_prompt(xl, xg, wc, bc, wrg, brg, wig, big, lam, *, tm):
    n, d = xl.shape
    row = pl.BlockSpec((tm, d), lambda i: (i, 0))
    last8 = pl.BlockSpec((SUBLANES, d), lambda i: (0, 0))
    consts = (wc, bc, wrg, brg, wig, big, lam)
    return pl.pallas_call(
        _lru_prompt_body, grid=(n // tm,),
        in_specs=[row, row] + [_const_spec(c.shape) for c in consts],
        out_specs=(row, last8, last8),
        out_shape=(jax.ShapeDtypeStruct((n, d), BF16),
                   jax.ShapeDtypeStruct((SUBLANES, d), F32),
                   jax.ShapeDtypeStruct((SUBLANES, d), F32)),
        scratch_shapes=[pltpu.VMEM((SUBLANES, d), F32), pltpu.VMEM((SUBLANES, d), F32)],
        compiler_params=_params(1), name="lru_prompt",
    )(xl, xg, *consts)


def _lru_sample_body(xl_ref, xg_ref, prev_ref, h0_ref, wc_ref, bc_ref, wrg_ref, brg_ref, wig_ref,
                     big_ref, lam_ref, o_ref, h_ref, *, seg):
    out, h = _lru_core(xl_ref[...], xg_ref[...], prev_ref[...], h0_ref[...], wc_ref[...], bc_ref[...],
                       wrg_ref[...], brg_ref[...], wig_ref[...], big_ref[...], lam_ref[...], seg=seg)
    o_ref[...] = out.astype(BF16)
    h_ref[...] = h


def _lru_sample(xl, xg, prev, h0, wc, bc, wrg, brg, wig, big, lam, *, tm, seg):
    n, d = xl.shape
    row = pl.BlockSpec((tm, d), lambda i: (i, 0))
    consts = (wc, bc, wrg, brg, wig, big, lam)
    return pl.pallas_call(
        functools.partial(_lru_sample_body, seg=seg), grid=(n // tm,),
        in_specs=[row, row, row, row] + [_const_spec(c.shape) for c in consts],
        out_specs=(row, row),
        out_shape=(jax.ShapeDtypeStruct((n, d), BF16), jax.ShapeDtypeStruct((n, d), F32)),
        compiler_params=_params(1), name="lru_sample",
    )(xl, xg, prev, h0, *consts)


def _ordered_bits_to_float(u):
    key = u ^ jnp.int32(-2 ** 31)
    bits = jnp.where(key >= 0, key, key ^ jnp.int32(0x7FFFFFFF))
    return lax.bitcast_convert_type(bits, F32)


def _kth_largest(count_ge, k_eff, rows):
    def step(i, u):
        cand = u | lax.shift_left(jnp.int32(1), 31 - i)
        ok = count_ge(_ordered_bits_to_float(cand)) >= k_eff
        return jnp.where(ok, cand, u)
    u = lax.fori_loop(0, 32, step, jnp.zeros((rows, 1), jnp.int32))
    return _ordered_bits_to_float(u)


def _tie_cut(count_lt, k_eff, rows, n_bits):
    def step(i, j):
        cand = j | lax.shift_left(jnp.int32(1), n_bits - 1 - i)
        ok = count_lt(cand.astype(F32)) < k_eff
        return jnp.where(ok, cand, j)
    j = lax.fori_loop(0, n_bits, step, jnp.zeros((rows, 1), jnp.int32))
    return j.astype(F32)


def _rank(score, thr, key_idx):
    return jnp.where(score > thr, -1.0, jnp.where(score == thr, key_idx, IDX_BIG))


def _lane_count(blk, pred, acc):
    for j in range(blk.shape[1] // LANES):
        acc = acc + jnp.where(pred(blk[:, j * LANES:(j + 1) * LANES]), 1, 0)
    return acc


def _pattn_body(iq_ref, ikw_ref, q_ref, ikt_ref, kt_ref, v_ref, o_ref,
                s_sc, m_sc, l_sc, acc_sc, *, tq, kc, n_sel, n_bits):
    qb = pl.program_id(0)
    n_chunks = ((qb + 1) * tq + kc - 1) // kc
    q_pos = qb * tq + lax.broadcasted_iota(jnp.int32, (tq, 1), 0)
    lane_idx = lax.broadcasted_iota(jnp.int32, (1, kc), 1)
    iw = ikw_ref[:, IDX_DIM:IDX_DIM + IDX_HEADS] * (IDX_DIM ** -0.5 * IDX_HEADS ** -0.5)

    def chunk_off(c):
        return pl.multiple_of(c * kc, kc)

    def score_chunk(c, carry):
        off = chunk_off(c)
        ikc = ikt_ref[:, pl.ds(off, kc)]
        acc = jnp.zeros((tq, kc), F32)
        for h in range(IDX_HEADS):
            acc = acc + jnp.maximum(_dot(iq_ref[h], ikc), 0.0) * iw[:, h:h + 1]
        s_sc[:, pl.ds(off, kc)] = jnp.where(off + lane_idx <= q_pos, acc, -jnp.inf)
        return carry
    lax.fori_loop(0, n_chunks, score_chunk, 0)

    def count_where(pred):
        def body(c, acc):
            return _lane_count(s_sc[:, pl.ds(chunk_off(c), kc)], pred, acc)
        acc = lax.fori_loop(0, n_chunks, body, jnp.zeros((tq, LANES), jnp.int32))
        return jnp.sum(acc, axis=1, keepdims=True)

    k_eff = jnp.minimum(q_pos + 1, n_sel)
    thr = _kth_largest(lambda t: count_where(lambda blk: blk >= t), k_eff, tq)

    def rank_chunk(c, carry):
        off = chunk_off(c)
        key_idx = (off + lane_idx).astype(F32)
        s_sc[:, pl.ds(off, kc)] = _rank(s_sc[:, pl.ds(off, kc)], thr, key_idx)
        return carry
    lax.fori_loop(0, n_chunks, rank_chunk, 0)
    cut = _tie_cut(lambda j: count_where(lambda blk: blk < j), k_eff, tq, n_bits)

    m_sc[...] = jnp.full_like(m_sc, NEG_BIG)
    l_sc[...] = jnp.zeros_like(l_sc)
    acc_sc[...] = jnp.zeros_like(acc_sc)
    c2 = HEAD_DIM ** -0.5 * LOG2E

    def attn_chunk(c, carry):
        off = chunk_off(c)
        sel = s_sc[:, pl.ds(off, kc)] <= cut
        for h in range(N_HEADS):
            g = h // GROUP
            s = _dot(q_ref[:, h * HEAD_DIM:(h + 1) * HEAD_DIM],
                     kt_ref[g * HEAD_DIM:(g + 1) * HEAD_DIM, pl.ds(off, kc)])
            s = jnp.where(sel, s, NEG_BIG)
            m_old = m_sc[h]
            m_new = jnp.maximum(m_old, jnp.max(s, axis=1, keepdims=True))
            alpha = jnp.exp2((m_old - m_new) * c2)
            p = jnp.exp2((s - m_new) * c2)
            l_sc[h] = alpha * l_sc[h] + jnp.sum(p, axis=1, keepdims=True)
            acc_sc[h] = alpha * acc_sc[h] + _dot(p.astype(BF16),
                                                 v_ref[pl.ds(off, kc), g * HEAD_DIM:(g + 1) * HEAD_DIM])
            m_sc[h] = m_new
        return carry
    lax.fori_loop(0, n_chunks, attn_chunk, 0)

    for h in range(N_HEADS):
        o_ref[:, h * HEAD_DIM:(h + 1) * HEAD_DIM] = (acc_sc[h] / l_sc[h]).astype(BF16)


def _prompt_attention(iq, ikw, q, ikt, kt, vb, *, tq, kc):
    n = q.shape[0]
    n_sel = min(TOPK_MAX, n // 4)
    n_bits = max(1, (n - 1).bit_length())
    body = functools.partial(_pattn_body, tq=tq, kc=kc, n_sel=n_sel, n_bits=n_bits)
    return pl.pallas_call(
        body, grid=(n // tq,),
        in_specs=[pl.BlockSpec((IDX_HEADS, tq, IDX_DIM), lambda i: (0, i, 0)),
                  pl.BlockSpec((tq, LANES), lambda i: (i, 0)),
                  pl.BlockSpec((tq, q.shape[1]), lambda i: (i, 0)),
                  _const_spec(ikt.shape), _const_spec(kt.shape), _const_spec(vb.shape)],
        out_specs=pl.BlockSpec((tq, q.shape[1]), lambda i: (i, 0)),
        out_shape=jax.ShapeDtypeStruct(q.shape, BF16),
        scratch_shapes=[pltpu.VMEM((tq, n), F32),
                        pltpu.VMEM((N_HEADS, tq, 1), F32),
                        pltpu.VMEM((N_HEADS, tq, 1), F32),
                        pltpu.VMEM((N_HEADS, tq, HEAD_DIM), F32)],
        compiler_params=_params(1), name="prompt_attention",
    )(iq, ikw, q, ikt, kt, vb)


def _sattn_body(pt_ref, iq_ref, iw_ref, ikn_ref, q_ref, kn_ref, vn_ref, cik_hbm, ck_hbm, cv_hbm,
                o_ref, ik_buf, k_buf, v_buf, s_sc, ik_sem, k_sem, v_sem,
                *, n_pages, page, cp, t_new, n_sel, n_bits):
    b = pl.program_id(0)
    nb = pl.num_programs(0)
    past = n_pages * page
    n_ch = n_pages // cp
    kc = cp * page
    rows = GROUP * t_new

    def ik_copy(seq, slot, p):
        return pltpu.make_async_copy(cik_hbm.at[pt_ref[seq * n_pages + p]],
                                     ik_buf.at[slot, pl.ds(p * page, page), :], ik_sem.at[slot])

    def kv_copy(hbm, buf, sem, seq, chunk, slot, p):
        return pltpu.make_async_copy(hbm.at[pt_ref[seq * n_pages + chunk * cp + p]],
                                     buf.at[slot, pl.ds(p * page, page), :], sem.at[slot])

    def start_ik(seq, slot):
        def body(p, c):
            ik_copy(seq, slot, p).start()
            return c
        lax.fori_loop(0, n_pages, body, 0)

    def wait_ik(seq, slot):
        def body(p, c):
            ik_copy(seq, slot, p).wait()
            return c
        lax.fori_loop(0, n_pages, body, 0)

    def start_kv(seq, chunk, slot):
        def body(p, c):
            kv_copy(ck_hbm, k_buf, k_sem, seq, chunk, slot, p).start()
            kv_copy(cv_hbm, v_buf, v_sem, seq, chunk, slot, p).start()
            return c
        lax.fori_loop(0, cp, body, 0)

    def wait_kv(seq, chunk, slot):
        def body(p, c):
            kv_copy(ck_hbm, k_buf, k_sem, seq, chunk, slot, p).wait()
            kv_copy(cv_hbm, v_buf, v_sem, seq, chunk, slot, p).wait()
            return c
        lax.fori_loop(0, cp, body, 0)

    islot = b % 2

    @pl.when(b == 0)
    def _():
        start_ik(0, 0)
        start_kv(0, 0, 0)

    @pl.when(b + 1 < nb)
    def _():
        start_ik(b + 1, 1 - islot)

    wait_ik(b, islot)

    iq = iq_ref[0]
    iw = iw_ref[0] * (IDX_DIM ** -0.5 * IDX_HEADS ** -0.5)

    def head_sum(x):
        acc = x[0:t_new]
        for h in range(1, IDX_HEADS):
            acc = acc + x[h * t_new:(h + 1) * t_new]
        return acc

    for c in range(n_ch):
        ikc = ik_buf[islot, c * kc:(c + 1) * kc, :].astype(BF16)
        s_sc[:, c * kc:(c + 1) * kc] = head_sum(jnp.maximum(_dot_nt(iq, ikc), 0.0) * iw)
    t_idx = lax.broadcasted_iota(jnp.int32, (t_new, LANES), 0)
    j_idx = lax.broadcasted_iota(jnp.int32, (t_new, LANES), 1)
    def pad_keys(x):
        return jnp.concatenate([x, jnp.zeros((LANES - t_new, x.shape[1]), F32)], axis=0).astype(BF16)

    s_new = head_sum(jnp.maximum(_dot_nt(iq, pad_keys(ikn_ref[0])), 0.0) * iw)
    s_new = jnp.where(j_idx <= t_idx, s_new, -jnp.inf)

    def count_past(pred):
        acc = jnp.zeros((t_new, LANES), jnp.int32)
        for c in range(n_ch):
            acc = _lane_count(s_sc[:, c * kc:(c + 1) * kc], pred, acc)
        return acc

    def count_ge(t):
        acc = count_past(lambda blk: blk >= t) + jnp.where(s_new >= t, 1, 0)
        return jnp.sum(acc, axis=1, keepdims=True)

    thr = _kth_largest(count_ge, n_sel, t_new)
    lane_f = lax.broadcasted_iota(jnp.int32, (1, kc), 1)
    for c in range(n_ch):
        s_sc[:, c * kc:(c + 1) * kc] = _rank(s_sc[:, c * kc:(c + 1) * kc], thr,
                                             (c * kc + lane_f).astype(F32))
    r_new = _rank(s_new, thr, (past + j_idx).astype(F32))

    def count_lt(j):
        acc = count_past(lambda blk: blk < j) + jnp.where(r_new < j, 1, 0)
        return jnp.sum(acc, axis=1, keepdims=True)

    cut = _tie_cut(count_lt, n_sel, t_new, n_bits)

    c2 = HEAD_DIM ** -0.5 * LOG2E
    cut_rows = jnp.tile(cut, (GROUP, 1))
    sel_new = jnp.tile(r_new, (GROUP, 1)) <= cut_rows
    m = []
    l = []
    acc = []
    for g in range(N_KV_HEADS):
        qg = q_ref[0, g]
        kn = pad_keys(kn_ref[0, :, g * HEAD_DIM:(g + 1) * HEAD_DIM])
        vn = pad_keys(vn_ref[0, :, g * HEAD_DIM:(g + 1) * HEAD_DIM])
        s = jnp.where(sel_new, _dot_nt(qg, kn), NEG_BIG)
        m_g = jnp.max(s, axis=1, keepdims=True)
        p = jnp.exp2((s - m_g) * c2)
        m.append(m_g)
        l.append(jnp.sum(p, axis=1, keepdims=True))
        acc.append(_dot(p.astype(BF16), vn))

    for c in range(n_ch):
        gidx = b * n_ch + c
        slot = gidx % 2
        if c + 1 < n_ch:
            start_kv(b, c + 1, 1 - slot)
        else:
            @pl.when(b + 1 < nb)
            def _():
                start_kv(b + 1, 0, 1 - slot)
        wait_kv(b, c, slot)
        sel = jnp.tile(s_sc[:, c * kc:(c + 1) * kc], (GROUP, 1)) <= cut_rows
        for g in range(N_KV_HEADS):
            kg = k_buf[slot, :, g * HEAD_DIM:(g + 1) * HEAD_DIM].astype(BF16)
            vg = v_buf[slot, :, g * HEAD_DIM:(g + 1) * HEAD_DIM].astype(BF16)
            s = jnp.where(sel, _dot_nt(q_ref[0, g], kg), NEG_BIG)
            m_new = jnp.maximum(m[g], jnp.max(s, axis=1, keepdims=True))
            alpha = jnp.exp2((m[g] - m_new) * c2)
            p = jnp.exp2((s - m_new) * c2)
            l[g] = alpha * l[g] + jnp.sum(p, axis=1, keepdims=True)
            acc[g] = alpha * acc[g] + _dot(p.astype(BF16), vg)
            m[g] = m_new

    for g in range(N_KV_HEADS):
        o_ref[0, g] = (acc[g] / l[g]).astype(BF16)


def _sample_attention(page_table, iq, iw, ikn, q, kn, vn, cik, ck, cv, *, cp):
    nb, n_pages = page_table.shape
    page = cik.shape[1]
    t_new = ikn.shape[1]
    past = n_pages * page
    n_sel = min(TOPK_MAX, (past + t_new) // 4)
    n_bits = max(1, (past + t_new - 1).bit_length())
    rows = GROUP * t_new
    d_kv = ck.shape[2]
    body = functools.partial(_sattn_body, n_pages=n_pages, page=page, cp=cp, t_new=t_new,
                             n_sel=n_sel, n_bits=n_bits)
    seq3 = lambda a: pl.BlockSpec((1,) + a.shape[1:], lambda i, pt: (i, 0, 0))
    seq4 = lambda a: pl.BlockSpec((1,) + a.shape[1:], lambda i, pt: (i, 0, 0, 0))
    any_spec = pl.BlockSpec(memory_space=pl.ANY)
    grid_spec = pltpu.PrefetchScalarGridSpec(
        num_scalar_prefetch=1, grid=(nb,),
        in_specs=[seq3(iq), seq3(iw), seq3(ikn), seq4(q), seq3(kn), seq3(vn),
                  any_spec, any_spec, any_spec],
        out_specs=pl.BlockSpec((1, N_KV_HEADS, rows, HEAD_DIM), lambda i, pt: (i, 0, 0, 0)),
        scratch_shapes=[pltpu.VMEM((2, past, IDX_DIM), F32),
                        pltpu.VMEM((2, cp * page, d_kv), F32),
                        pltpu.VMEM((2, cp * page, d_kv), F32),
                        pltpu.VMEM((t_new, past), F32),
                        pltpu.SemaphoreType.DMA((2,)),
                        pltpu.SemaphoreType.DMA((2,)),
                        pltpu.SemaphoreType.DMA((2,))],
    )
    return pl.pallas_call(
        body, grid_spec=grid_spec,
        out_shape=jax.ShapeDtypeStruct((nb, N_KV_HEADS, rows, HEAD_DIM), BF16),
        compiler_params=_params(1), name="sample_attention",
    )(page_table.reshape(-1), iq, iw, ikn, q, kn, vn, cik, ck, cv)


def _tail_body(x_ref, oa_ref, ol_ref, ga_ref, gb_ref, wua_ref, wul_ref, wo_ref, w1_ref, w2_ref,
               gpost_ref, gfpre_ref, gfpost_ref, y_ref):
    merged = (jax.nn.sigmoid(ga_ref[...]) * _dot(oa_ref[...], wua_ref[...])
              + jax.nn.sigmoid(gb_ref[...]) * _dot(ol_ref[...], wul_ref[...]))
    x = x_ref[...] + _rmsnorm(_dot(merged.astype(BF16), wo_ref[...]), gpost_ref[...])
    hf = _rmsnorm(x, gfpre_ref[...]).astype(BF16)
    f = jnp.square(jnp.maximum(_dot(hf, w1_ref[...]), 0.0)).astype(BF16)
    y_ref[...] = x + _rmsnorm(_dot(f, w2_ref[...]), gfpost_ref[...])


def _tail(x, oa, ol, ga, gb, wua, wul, wo, w1, w2, gpost, gfpre, gfpost, *, tm):
    n, d = x.shape
    row = lambda a: pl.BlockSpec((tm, a.shape[1]), lambda i: (i, 0))
    consts = (wua, wul, wo, w1, w2, gpost, gfpre, gfpost)
    return pl.pallas_call(
        _tail_body, grid=(n // tm,),
        in_specs=[row(x), row(oa), row(ol), row(ga), row(gb)] + [_const_spec(c.shape) for c in consts],
        out_specs=row(x), out_shape=jax.ShapeDtypeStruct((n, d), F32),
        compiler_params=_params(1), name="merge_ffn_tail",
    )(x, oa, ol, ga, gb, *consts)


def _row_tile(n, want):
    t = min(want, n)
    while n % t:
        t //= 2
    return t


def _layer(xp, xs, ck, cv, cik, h0, conv0, page_table, w):
    bp, sp, d_model = xp.shape
    nb, t_new, _ = xs.shape
    assert bp == 1, "prompt group is a single sequence"
    assert t_new == SUBLANES, "one sample sequence must fill one 8-row group"
    d_attn = N_HEADS * HEAD_DIM
    d_kv = N_KV_HEADS * HEAD_DIM
    d_lru = w["w_conv"].shape[1]
    dims = dict(d_attn=d_attn, d_kv=d_kv, d_lru=d_lru)

    w_in = w["w_in"]
    o_iq_end = d_attn + 2 * d_kv + IDX_HEADS * IDX_DIM
    o_iw_end = o_iq_end + IDX_HEADS
    o_ik_end = o_iw_end + IDX_DIM
    wa = w_in[:, :o_iq_end].astype(BF16)
    wi = jnp.concatenate([w_in[:, o_iw_end:o_ik_end], w_in[:, o_iq_end:o_iw_end],
                          jnp.zeros((d_model, LANES - IDX_DIM - IDX_HEADS), F32)], axis=1).astype(BF16)
    wl = w_in[:, o_ik_end:].astype(BF16)
    lru_consts = (w["w_conv"], w["b_conv"][None], w["w_rg"].astype(BF16), w["b_rg"][None],
                  w["w_ig"].astype(BF16), w["b_ig"][None], w["lru_lambda"][None])
    tail_consts = (w["w_up_attn"].astype(BF16), w["w_up_lru"].astype(BF16), w["w_out"].astype(BF16),
                   w["w_ff1"].astype(BF16), w["w_ff2"].astype(BF16),
                   w["g_mix_post"][None], w["g_ffn_pre"][None], w["g_ffn_post"][None])
    g_pre = w["g_mix_pre"][None]

    ns = nb * t_new
    xs2 = xs.reshape(ns, d_model)
    tms = _row_tile(ns, 256)
    q, k, v, kb, vb, iq, ikw, xl, xg, ga, gb = _in_projection(xs2, g_pre, wa, wi, wl, tm=tms, **dims)
    prev = jnp.pad(conv0, ((0, 0), (SUBLANES - (CONV_W - 1), 0), (0, 0))).reshape(ns, d_lru)
    h0x = jnp.broadcast_to(h0[:, None, :], (nb, t_new, d_lru)).reshape(ns, d_lru)
    o_lru, h_all = _lru_sample(xl, xg, prev, h0x, *lru_consts, tm=tms, seg=t_new)
    iq_s = iq.reshape(IDX_HEADS, nb, t_new, IDX_DIM).transpose(1, 0, 2, 3).reshape(nb, IDX_HEADS * t_new, IDX_DIM)
    iw_s = ikw[:, IDX_DIM:IDX_DIM + IDX_HEADS].reshape(nb, t_new, IDX_HEADS).transpose(0, 2, 1)
    iw_s = iw_s.reshape(nb, IDX_HEADS * t_new, 1)
    ikn = ikw[:, :IDX_DIM].reshape(nb, t_new, IDX_DIM)
    q_s = q.reshape(nb, t_new, N_KV_HEADS, GROUP, HEAD_DIM).transpose(0, 2, 3, 1, 4)
    q_s = q_s.reshape(nb, N_KV_HEADS, GROUP * t_new, HEAD_DIM)
    n_pool, page = cik.shape[0], cik.shape[1]
    n_pages = page_table.shape[1]
    o_att = _sample_attention(page_table, iq_s, iw_s, ikn, q_s,
                              k.reshape(nb, t_new, d_kv), v.reshape(nb, t_new, d_kv),
                              cik, ck.reshape(n_pool, page, d_kv), cv.reshape(n_pool, page, d_kv),
                              cp=math.gcd(n_pages, 16))
    o_att = o_att.reshape(nb, N_KV_HEADS, GROUP, t_new, HEAD_DIM).transpose(0, 3, 1, 2, 4).reshape(ns, d_attn)
    ys = _tail(xs2, o_att, o_lru, ga, gb, *tail_consts, tm=tms).reshape(nb, t_new, d_model)
    st_s = (k.reshape(nb, t_new, N_KV_HEADS, HEAD_DIM), v.reshape(nb, t_new, N_KV_HEADS, HEAD_DIM), ikn,
            h_all.reshape(nb, t_new, d_lru)[:, -1], xl.reshape(nb, t_new, d_lru)[:, t_new - (CONV_W - 1):])

    xp2 = xp.reshape(sp, d_model)
    tmp = _row_tile(sp, 256)
    q, k, v, kb, vb, iq, ikw, xl, xg, ga, gb = _in_projection(xp2, g_pre, wa, wi, wl, tm=tmp, **dims)
    o_lru, h_last, tail3 = _lru_prompt(xl, xg, *lru_consts, tm=tmp)
    ik = ikw[:, :IDX_DIM]
    o_att = _prompt_attention(iq, ikw, q, ik.astype(BF16).T, kb.T, vb, tq=_row_tile(sp, 128),
                              kc=_row_tile(sp, 512))
    yp = _tail(xp2, o_att, o_lru, ga, gb, *tail_consts, tm=tmp).reshape(1, sp, d_model)
    st_p = (k.reshape(1, sp, N_KV_HEADS, HEAD_DIM), v.reshape(1, sp, N_KV_HEADS, HEAD_DIM),
            ik.reshape(1, sp, IDX_DIM), h_last[SUBLANES - 1:], tail3[None, SUBLANES - (CONV_W - 1):])
    return yp, ys, st_p, st_s


def kernel(x_prompt, x_sample, cache_k, cache_v, cache_idx_k, state_h, state_conv, page_table, w_in, w_conv, b_conv, w_rg, b_rg, w_ig, b_ig, lru_lambda, w_up_attn, w_up_lru, w_out, g_mix_pre, g_mix_post, g_ffn_pre, g_ffn_post, w_ff1, w_ff2):
    names = ("w_in", "w_conv", "b_conv", "w_rg", "b_rg", "w_ig", "b_ig", "lru_lambda", "w_up_attn",
             "w_up_lru", "w_out", "g_mix_pre", "g_mix_post", "g_ffn_pre", "g_ffn_post", "w_ff1", "w_ff2")
    stacked = (w_in, w_conv, b_conv, w_rg, b_rg, w_ig, b_ig, lru_lambda, w_up_attn, w_up_lru, w_out,
               g_mix_pre, g_mix_post, g_ffn_pre, g_ffn_post, w_ff1, w_ff2)
    yp, ys = x_prompt, x_sample
    new_p, new_s = [], []
    for layer in range(w_in.shape[0]):
        w = {n: a[layer] for n, a in zip(names, stacked)}
        yp, ys, st_p, st_s = _layer(yp, ys, cache_k[layer], cache_v[layer], cache_idx_k[layer],
                                    state_h[layer], state_conv[layer], page_table, w)
        new_p.append(st_p)
        new_s.append(st_s)
    stack = lambda states, j: jnp.stack([s[j] for s in states], axis=0)
    return (yp, ys) + tuple(stack(new_p, j) for j in range(5)) + tuple(stack(new_s, j) for j in range(5))
```

```python
import functools
import math

import jax
import jax.numpy as jnp
from jax import lax
from jax.experimental import pallas as pl
from jax.experimental.pallas import tpu as pltpu

N_HEADS = 8
HEAD_DIM = 128
N_KV_HEADS = 2
GROUP = N_HEADS // N_KV_HEADS
IDX_HEADS = 8
IDX_DIM = 64
TOPK_MAX = 256
LRU_BLOCKS = 8
CONV_W = 4
LRU_C = 8.0
EPS = 1e-6

LANES = 128
SUBLANES = 8
VMEM_LIMIT = 56 * 1024 * 1024

F32 = jnp.float32
BF16 = jnp.bfloat16
NEG_BIG = -1e30
IDX_BIG = 1e9
LOG2E = math.log2(math.e)


def _const_spec(shape):
    nd = len(shape)
    return pl.BlockSpec(shape, lambda *_: (0,) * nd, pipeline_mode=pl.Buffered(1))


def _params(n_grid):
    return pltpu.CompilerParams(dimension_semantics=("arbitrary",) * n_grid,
                                vmem_limit_bytes=VMEM_LIMIT)


def _rmsnorm(x, g):
    return x * lax.rsqrt(jnp.mean(x * x, axis=-1, keepdims=True) + EPS) * g


def _dot(a, b):
    return jnp.dot(a, b, preferred_element_type=F32)


def _dot_nt(a, b):
    return lax.dot_general(a, b, (((1,), (1,)), ((), ())), preferred_element_type=F32)


def _inproj_body(x_ref, g_ref, wa_ref, wi_ref, wl_ref,
                 q_ref, k_ref, v_ref, kb_ref, vb_ref, iq_ref, ikw_ref, xl_ref, xg_ref, ga_ref, gb_ref,
                 *, d_attn, d_kv, d_lru, d_model):
    hn = _rmsnorm(x_ref[...], g_ref[...]).astype(BF16)
    q_ref[...] = _dot(hn, wa_ref[:, :d_attn]).astype(BF16)
    k = _dot(hn, wa_ref[:, d_attn:d_attn + d_kv])
    k_ref[...] = k
    kb_ref[...] = k.astype(BF16)
    v = _dot(hn, wa_ref[:, d_attn + d_kv:d_attn + 2 * d_kv])
    v_ref[...] = v
    vb_ref[...] = v.astype(BF16)
    iq = _dot(hn, wa_ref[:, d_attn + 2 * d_kv:])
    for h in range(IDX_HEADS):
        iq_ref[h] = iq[:, h * IDX_DIM:(h + 1) * IDX_DIM].astype(BF16)
    ikw_ref[...] = _dot(hn, wi_ref[...])
    xl_ref[...] = _dot(hn, wl_ref[:, :d_lru])
    xg_ref[...] = _dot(hn, wl_ref[:, d_lru:2 * d_lru])
    ga_ref[...] = _dot(hn, wl_ref[:, 2 * d_lru:2 * d_lru + d_model])
    gb_ref[...] = _dot(hn, wl_ref[:, 2 * d_lru + d_model:])


def _in_projection(x, g, wa, wi, wl, *, d_attn, d_kv, d_lru, tm):
    n, d_model = x.shape
    grid = (n // tm,)
    row = lambda w: pl.BlockSpec((tm, w), lambda i: (i, 0))
    out_shape = (
        jax.ShapeDtypeStruct((n, d_attn), BF16),
        jax.ShapeDtypeStruct((n, d_kv), F32),
        jax.ShapeDtypeStruct((n, d_kv), F32),
        jax.ShapeDtypeStruct((n, d_kv), BF16),
        jax.ShapeDtypeStruct((n, d_kv), BF16),
        jax.ShapeDtypeStruct((IDX_HEADS, n, IDX_DIM), BF16),
        jax.ShapeDtypeStruct((n, LANES), F32),
        jax.ShapeDtypeStruct((n, d_lru), F32),
        jax.ShapeDtypeStruct((n, d_lru), F32),
        jax.ShapeDtypeStruct((n, d_model), F32),
        jax.ShapeDtypeStruct((n, d_model), F32),
    )
    out_specs = (row(d_attn), row(d_kv), row(d_kv), row(d_kv), row(d_kv),
                 pl.BlockSpec((IDX_HEADS, tm, IDX_DIM), lambda i: (0, i, 0)),
                 row(LANES), row(d_lru), row(d_lru), row(d_model), row(d_model))
    body = functools.partial(_inproj_body, d_attn=d_attn, d_kv=d_kv, d_lru=d_lru, d_model=d_model)
    return pl.pallas_call(
        body, grid=grid,
        in_specs=[row(d_model), _const_spec(g.shape), _const_spec(wa.shape), _const_spec(wi.shape),
                  _const_spec(wl.shape)],
        out_specs=out_specs, out_shape=out_shape, compiler_params=_params(1), name="in_projection",
    )(x, g, wa, wi, wl)


def _shift_rows(cur, prev, j, row_in_seg):
    tm = cur.shape[0]
    from_prev = pltpu.roll(prev, tm - SUBLANES + j, axis=0)
    return jnp.where(row_in_seg < j, from_prev, pltpu.roll(cur, j, axis=0))


def _lru_core(xl, xg, prev, h_in, wc, bc, wrg, brg, wig, big, lam, *, seg):
    tm, d = xl.shape
    row = lax.broadcasted_iota(jnp.int32, (tm, 1), 0)
    row_in_seg = row % seg
    u = bc + wc[CONV_W - 1:CONV_W] * xl
    for j in range(1, CONV_W):
        u = u + wc[CONV_W - 1 - j:CONV_W - j] * _shift_rows(xl, prev, j, row_in_seg)
    ub = u.astype(BF16)
    blk = d // LRU_BLOCKS
    r = jnp.concatenate([_dot(ub[:, n * blk:(n + 1) * blk], wrg[n]) for n in range(LRU_BLOCKS)], axis=1)
    ig = jnp.concatenate([_dot(ub[:, n * blk:(n + 1) * blk], wig[n]) for n in range(LRU_BLOCKS)], axis=1)
    r = jax.nn.sigmoid(r + brg)
    ig = jax.nn.sigmoid(ig + big)
    softplus_neg = jnp.maximum(-lam, 0.0) + jnp.log1p(jnp.exp(-jnp.abs(lam)))
    log_a = -LRU_C * r * softplus_neg
    a = jnp.exp(log_a)
    b = jnp.sqrt(-jnp.tanh(log_a) * (a * a + 1.0)) * (ig * u)
    d_step = 1
    while d_step < seg:
        keep = row_in_seg >= d_step
        a_sh = jnp.where(keep, pltpu.roll(a, d_step, axis=0), 1.0)
        b_sh = jnp.where(keep, pltpu.roll(b, d_step, axis=0), 0.0)
        b = a * b_sh + b
        a = a * a_sh
        d_step *= 2
    h = a * h_in + b
    gelu = 0.5 * xg * (1.0 + jnp.tanh(math.sqrt(2.0 / math.pi) * (xg + 0.044715 * (xg * xg * xg))))
    return h * gelu, h


def _lru_prompt_body(xl_ref, xg_ref, wc_ref, bc_ref, wrg_ref, brg_ref, wig_ref, big_ref, lam_ref,
                     o_ref, h_ref, tail_ref, hc_sc, prev_sc):
    @pl.when(pl.program_id(0) == 0)
    def _():
        hc_sc[...] = jnp.zeros_like(hc_sc)
        prev_sc[...] = jnp.zeros_like(prev_sc)

    xl = xl_ref[...]
    tm = xl.shape[0]
    prev = jnp.tile(prev_sc[...], (tm // SUBLANES, 1))
    out, h = _lru_core(xl, xg_ref[...], prev, hc_sc[0:1, :], wc_ref[...], bc_ref[...], wrg_ref[...],
                       brg_ref[...], wig_ref[...], big_ref[...], lam_ref[...], seg=tm)
    o_ref[...] = out.astype(BF16)
    last = h[tm - SUBLANES:, :]
    hc_sc[...] = jnp.broadcast_to(last[SUBLANES - 1:SUBLANES, :], hc_sc.shape)
    h_ref[...] = last
    prev_sc[...] = xl[tm - SUBLANES:, :]
    tail_ref[...] = xl[tm - SUBLANES:, :]


def _lru_prompt(xl, xg, wc, bc, wrg, brg, wig, big, lam, *, tm):
    n, d = xl.shape
    row = pl.BlockSpec((tm, d), lambda i: (i, 0))
    last8 = pl.BlockSpec((SUBLANES, d), lambda i: (0, 0))
    consts = (wc, bc, wrg, brg, wig, big, lam)
    return pl.pallas_call(
        _lru_prompt_body, grid=(n // tm,),
        in_specs=[row, row] + [_const_spec(c.shape) for c in consts],
        out_specs=(row, last8, last8),
        out_shape=(jax.ShapeDtypeStruct((n, d), BF16),
                   jax.ShapeDtypeStruct((SUBLANES, d), F32),
                   jax.ShapeDtypeStruct((SUBLANES, d), F32)),
        scratch_shapes=[pltpu.VMEM((SUBLANES, d), F32), pltpu.VMEM((SUBLANES, d), F32)],
        compiler_params=_params(1), name="lru_prompt",
    )(xl, xg, *consts)


def _lru_sample_body(xl_ref, xg_ref, prev_ref, h0_ref, wc_ref, bc_ref, wrg_ref, brg_ref, wig_ref,
                     big_ref, lam_ref, o_ref, h_ref, *, seg):
    out, h = _lru_core(xl_ref[...], xg_ref[...], prev_ref[...], h0_ref[...], wc_ref[...], bc_ref[...],
                       wrg_ref[...], brg_ref[...], wig_ref[...], big_ref[...], lam_ref[...], seg=seg)
    o_ref[...] = out.astype(BF16)
    h_ref[...] = h


def _lru_sample(xl, xg, prev, h0, wc, bc, wrg, brg, wig, big, lam, *, tm, seg):
    n, d = xl.shape
    row = pl.BlockSpec((tm, d), lambda i: (i, 0))
    consts = (wc, bc, wrg, brg, wig, big, lam)
    return pl.pallas_call(
        functools.partial(_lru_sample_body, seg=seg), grid=(n // tm,),
        in_specs=[row, row, row, row] + [_const_spec(c.shape) for c in consts],
        out_specs=(row, row),
        out_shape=(jax.ShapeDtypeStruct((n, d), BF16), jax.ShapeDtypeStruct((n, d), F32)),
        compiler_params=_params(1), name="lru_sample",
    )(xl, xg, prev, h0, *consts)


def _ordered_bits_to_float(u):
    key = u ^ jnp.int32(-2 ** 31)
    bits = jnp.where(key >= 0, key, key ^ jnp.int32(0x7FFFFFFF))
    return lax.bitcast_convert_type(bits, F32)


def _float_to_ordered_bits(x):
    bits = lax.bitcast_convert_type(x, jnp.int32)
    key = jnp.where(bits >= 0, bits, bits ^ jnp.int32(0x7FFFFFFF))
    return key ^ jnp.int32(-2 ** 31)


def _kth_search(count_ge, k_eff, bracket=None):
    shape = k_eff.shape
    zero = jnp.zeros(shape, jnp.int32)
    if bracket is None:
        i0, u0 = jnp.int32(0), zero
    else:
        u_lo, u_hi = (_float_to_ordered_bits(x) for x in bracket)
        i0 = jnp.minimum(jnp.min(lax.clz(u_lo ^ u_hi)), 31)
        keep = lax.shift_left(lax.shift_left(jnp.int32(-1), 31 - i0), 1)
        u0 = u_hi & keep

    def cond(st):
        return (st[0] < 32) & (st[4] == 0)

    def body(st):
        i, u, thr, exact, _ = st
        cand = u | lax.shift_left(jnp.int32(1), 31 - i)
        t = _ordered_bits_to_float(cand)
        c = count_ge(t)
        hit = (c == k_eff) & (exact == 0)
        thr = jnp.where(hit, t, thr)
        exact = jnp.where(hit, 1, exact)
        u = jnp.where(c >= k_eff, cand, u)
        return i + 1, u, thr, exact, jnp.min(exact)

    init = (i0, u0, jnp.zeros(shape, F32), zero, jnp.int32(0))
    _, u, thr, exact, _ = lax.while_loop(cond, body, init)
    return jnp.where(exact == 1, thr, _ordered_bits_to_float(u)), exact


def _tie_cut(count_lt, k_eff, n_bits):
    def step(i, j):
        cand = j | lax.shift_left(jnp.int32(1), n_bits - 1 - i)
        ok = count_lt(cand.astype(F32)) < k_eff
        return jnp.where(ok, cand, j)
    j = lax.fori_loop(0, n_bits, step, jnp.zeros(k_eff.shape, jnp.int32))
    return j.astype(F32)


def _rank(score, thr, key_idx):
    return jnp.where(score > thr, -1.0, jnp.where(score == thr, key_idx, IDX_BIG))


def _tree_sum(terms):
    while len(terms) > 1:
        terms = [a + b for a, b in zip(terms[0::2], terms[1::2])] + ([terms[-1]] if len(terms) % 2 else [])
    return terms[0]


COUNT_ROWS = 64


def _pattn_body(iq2_ref, iwt_ref, q2_ref, ik_ref, k_ref, vt_ref, o_ref,
                s_sc, cm_sc, sa_sc, sb_sc, pa_sc, pb_sc, thr_sc, m_sc, l_sc, acc_sc,
                *, tq, kc, n_sel, n_bits):
    qb = pl.program_id(0)
    n_chunks = ((qb + 1) * tq + kc - 1) // kc
    n_chunks_even = (n_chunks + 1) // 2 * 2
    q_pos = qb * tq + lax.broadcasted_iota(jnp.int32, (1, tq), 1)
    n_pairs = N_HEADS // 2
    iw = iwt_ref[...] * (IDX_DIM ** -0.5 * IDX_HEADS ** -0.5)

    def chunk_off(c):
        return pl.multiple_of(c * kc, kc)

    def score_chunk(c, carry):
        off = chunk_off(c)
        ikc = ik_ref[pl.ds(off, kc), :]
        acc = jnp.zeros((kc, tq), F32)
        for hp in range(IDX_HEADS // 2):
            lg = _dot(ikc, iq2_ref[0, hp])
            acc = acc + jnp.maximum(lg[:, :tq], 0.0) * iw[2 * hp:2 * hp + 1, :]
            acc = acc + jnp.maximum(lg[:, tq:], 0.0) * iw[2 * hp + 1:2 * hp + 2, :]
        k_pos = off + lax.broadcasted_iota(jnp.int32, (kc, 1), 0)
        sc = jnp.where(k_pos <= q_pos, acc, -jnp.inf)
        s_sc[pl.ds(off, kc), :] = sc
        cm_sc[...] = functools.reduce(jnp.maximum, [cm_sc[...]] + [
            sc[j * TOPK_MAX:(j + 1) * TOPK_MAX] for j in range(kc // TOPK_MAX)])
        return carry
    cm_sc[...] = jnp.full_like(cm_sc, -jnp.inf)
    lax.fori_loop(0, n_chunks_even, score_chunk, 0)
    bracket = (jnp.min(cm_sc[...], axis=0, keepdims=True), jnp.max(cm_sc[...], axis=0, keepdims=True))

    def count_where(pred):
        def body(c, acc):
            off = chunk_off(c)
            for j in range(kc // COUNT_ROWS):
                acc = acc + jnp.where(pred(s_sc[pl.ds(off + j * COUNT_ROWS, COUNT_ROWS), :]), 1, 0)
            return acc
        acc = lax.fori_loop(0, n_chunks, body, jnp.zeros((COUNT_ROWS, tq), jnp.int32))
        return jnp.sum(acc, axis=0, keepdims=True)

    k_eff = jnp.minimum(q_pos + 1, n_sel)
    thr, exact = _kth_search(lambda t: count_where(lambda blk: blk >= t), k_eff, bracket)
    thr_sc[...] = jnp.broadcast_to(thr, thr_sc.shape)

    @pl.when(jnp.min(exact) == 0)
    def _():
        def rank_chunk(c, carry):
            off = chunk_off(c)
            key_idx = (off + lax.broadcasted_iota(jnp.int32, (kc, 1), 0)).astype(F32)
            s_sc[pl.ds(off, kc), :] = _rank(s_sc[pl.ds(off, kc), :], thr, key_idx)
            return carry
        lax.fori_loop(0, n_chunks, rank_chunk, 0)
        cut = _tie_cut(lambda j: count_where(lambda blk: blk < j), k_eff, n_bits)

        def recode_chunk(c, carry):
            off = chunk_off(c)
            s_sc[pl.ds(off, kc), :] = jnp.where(s_sc[pl.ds(off, kc), :] <= cut, 1.0, -1.0)
            return carry
        lax.fori_loop(0, n_chunks, recode_chunk, 0)
        thr_sc[...] = jnp.zeros_like(thr_sc)

    sel_thr = thr_sc[0:1, :]
    m_sc[...] = jnp.full_like(m_sc, NEG_BIG)
    l_sc[...] = jnp.zeros_like(l_sc)
    acc_sc[...] = jnp.zeros_like(acc_sc)
    c2 = HEAD_DIM ** -0.5 * LOG2E
    last_chunk = k_ref.shape[0] // kc - 1

    def qk(c, s_buf):
        off = chunk_off(jnp.minimum(c, last_chunk))
        for hp in range(n_pairs):
            g = (2 * hp) // GROUP
            s_buf[:, 2 * hp * tq:(2 * hp + 2) * tq] = _dot(
                k_ref[pl.ds(off, kc), g * HEAD_DIM:(g + 1) * HEAD_DIM], q2_ref[0, hp])

    def pv(c, p_buf, alphas):
        off = chunk_off(jnp.maximum(c, 0))
        for hp in range(n_pairs):
            g = (2 * hp) // GROUP
            acc_sc[hp] = alphas[hp] * acc_sc[hp] + _dot(
                vt_ref[g * HEAD_DIM:(g + 1) * HEAD_DIM, pl.ds(off, kc)],
                p_buf[:, 2 * hp * tq:(2 * hp + 2) * tq])

    def softmax(c, s_buf, p_buf):
        bias = jnp.where(s_sc[pl.ds(chunk_off(c), kc), :] >= sel_thr, 0.0, NEG_BIG)
        alphas = []
        for h in range(N_HEADS):
            s = s_buf[:, h * tq:(h + 1) * tq] + bias
            s_buf[:, h * tq:(h + 1) * tq] = s
            m_old = m_sc[h]
            m_new = jnp.maximum(m_old, jnp.max(s, axis=0, keepdims=True))
            alpha = jnp.exp2((m_old - m_new) * c2)
            p = jnp.exp2((s_buf[:, h * tq:(h + 1) * tq] - m_new[0:1, :]) * c2)
            l_sc[h] = alpha * l_sc[h] + jnp.sum(p, axis=0, keepdims=True)
            m_sc[h] = m_new
            p_buf[:, h * tq:(h + 1) * tq] = p.astype(BF16)
            alphas.append(alpha[0:1, :])
        return [jnp.concatenate(alphas[2 * hp:2 * hp + 2], axis=1) for hp in range(n_pairs)]

    def stage(c, s_cur, p_cur, s_next, p_prev, alphas_prev):
        qk(c + 1, s_next)
        pv(c - 1, p_prev, alphas_prev)
        return softmax(c, s_cur, p_cur)

    qk(0, sa_sc)
    pb_sc[...] = jnp.zeros_like(pb_sc)

    def attn_two_chunks(i, alphas):
        alphas = stage(2 * i, sa_sc, pa_sc, sb_sc, pb_sc, alphas)
        return stage(2 * i + 1, sb_sc, pb_sc, sa_sc, pa_sc, alphas)
    ones = [jnp.ones((1, 2 * tq), F32)] * n_pairs
    alphas = lax.fori_loop(0, n_chunks_even // 2, attn_two_chunks, ones)
    pv(n_chunks_even - 1, pb_sc, alphas)

    for h in range(N_HEADS):
        o_t = acc_sc[h // 2][:, (h % 2) * tq:(h % 2 + 1) * tq] / l_sc[h][0:1, :]
        o_ref[:, h * HEAD_DIM:(h + 1) * HEAD_DIM] = o_t.T.astype(BF16)


def _prompt_attention(iq2, iwt, q2, ik, kb, vt, *, tq, kc):
    n = kb.shape[0]
    d_attn = N_HEADS * HEAD_DIM
    n_sel = min(TOPK_MAX, n // 4)
    n_bits = max(1, (n - 1).bit_length())
    assert (n // kc) % 2 == 0, "the attention pipeline consumes key chunks in pairs"
    assert kc % TOPK_MAX == 0, "a key chunk holds whole sets of residue classes"
    body = functools.partial(_pattn_body, tq=tq, kc=kc, n_sel=n_sel, n_bits=n_bits)
    blk4 = lambda a: pl.BlockSpec((1,) + a.shape[1:], lambda i: (i, 0, 0, 0))
    return pl.pallas_call(
        body, grid=(n // tq,),
        in_specs=[blk4(iq2), pl.BlockSpec((IDX_HEADS, tq), lambda i: (0, i)), blk4(q2),
                  _const_spec(ik.shape), _const_spec(kb.shape), _const_spec(vt.shape)],
        out_specs=pl.BlockSpec((tq, d_attn), lambda i: (i, 0)),
        out_shape=jax.ShapeDtypeStruct((n, d_attn), BF16),
        scratch_shapes=[pltpu.VMEM((n, tq), F32),
                        pltpu.VMEM((TOPK_MAX, tq), F32),
                        pltpu.VMEM((kc, N_HEADS * tq), F32),
                        pltpu.VMEM((kc, N_HEADS * tq), F32),
                        pltpu.VMEM((kc, N_HEADS * tq), BF16),
                        pltpu.VMEM((kc, N_HEADS * tq), BF16),
                        pltpu.VMEM((SUBLANES, tq), F32),
                        pltpu.VMEM((N_HEADS, SUBLANES, tq), F32),
                        pltpu.VMEM((N_HEADS, SUBLANES, tq), F32),
                        pltpu.VMEM((N_HEADS // 2, HEAD_DIM, 2 * tq), F32)],
        compiler_params=_params(1), name="prompt_attention",
    )(iq2, iwt, q2, ik, kb, vt)


def _sattn_body(pt_ref, iq_ref, iw_ref, ikn_ref, q_ref, kn_ref, vn_ref, cik_hbm, ck_hbm, cv_hbm,
                o_ref, ik_buf, k_buf, v_buf, s_sc, snew_sc, thr_sc, ik_sem, k_sem, v_sem,
                *, n_pages, page, cp, t_new, n_sel, n_bits, page_base):
    b = pl.program_id(0)
    nb = pl.num_programs(0)
    past = n_pages * page
    n_ch = n_pages // cp
    kc = cp * page

    def page_id(seq, p):
        return pt_ref[seq * n_pages + p] + page_base

    def ik_copy(seq, slot, p):
        return pltpu.make_async_copy(cik_hbm.at[page_id(seq, p)],
                                     ik_buf.at[slot, pl.ds(p * page, page), :], ik_sem.at[slot])

    def kv_copy(hbm, buf, sem, seq, chunk, slot, p):
        return pltpu.make_async_copy(hbm.at[page_id(seq, chunk * cp + p)],
                                     buf.at[slot, pl.ds(p * page, page), :], sem.at[slot])

    def start_ik(seq, slot):
        def body(p, c):
            ik_copy(seq, slot, p).start()
            return c
        lax.fori_loop(0, n_pages, body, 0)

    def wait_ik(seq, slot):
        def body(p, c):
            ik_copy(seq, slot, p).wait()
            return c
        lax.fori_loop(0, n_pages, body, 0)

    def start_kv(seq, chunk, slot):
        def body(p, c):
            kv_copy(ck_hbm, k_buf, k_sem, seq, chunk, slot, p).start()
            kv_copy(cv_hbm, v_buf, v_sem, seq, chunk, slot, p).start()
            return c
        lax.fori_loop(0, cp, body, 0)

    def wait_kv(seq, chunk, slot):
        def body(p, c):
            kv_copy(ck_hbm, k_buf, k_sem, seq, chunk, slot, p).wait()
            kv_copy(cv_hbm, v_buf, v_sem, seq, chunk, slot, p).wait()
            return c
        lax.fori_loop(0, cp, body, 0)

    islot = b % 2

    @pl.when(b == 0)
    def _():
        start_ik(0, 0)
        start_kv(0, 0, 0)

    @pl.when(b + 1 < nb)
    def _():
        start_ik(b + 1, 1 - islot)

    wait_ik(b, islot)

    iq = iq_ref[0]
    iw = iw_ref[0] * (IDX_DIM ** -0.5 * IDX_HEADS ** -0.5)

    def head_sum(x):
        return _tree_sum([x[h * t_new:(h + 1) * t_new] for h in range(IDX_HEADS)])

    for c in range(n_ch):
        ikc = ik_buf[islot, c * kc:(c + 1) * kc, :].astype(BF16)
        s_sc[:, c * kc:(c + 1) * kc] = head_sum(jnp.maximum(_dot_nt(iq, ikc), 0.0) * iw)
    t_idx = lax.broadcasted_iota(jnp.int32, (t_new, LANES), 0)
    j_idx = lax.broadcasted_iota(jnp.int32, (t_new, LANES), 1)

    def pad_keys(x):
        return jnp.concatenate([x, jnp.zeros((LANES - t_new, x.shape[1]), F32)], axis=0).astype(BF16)

    s_new = head_sum(jnp.maximum(_dot_nt(iq, pad_keys(ikn_ref[0])), 0.0) * iw)
    snew_sc[...] = jnp.where(j_idx <= t_idx, s_new, -jnp.inf)

    def count_where(pred):
        terms = [jnp.where(pred(snew_sc[...]), 1, 0)]
        for j in range(past // LANES):
            terms.append(jnp.where(pred(s_sc[:, j * LANES:(j + 1) * LANES]), 1, 0))
        return jnp.sum(_tree_sum(terms), axis=1, keepdims=True)

    k_eff = jnp.full((t_new, 1), n_sel, jnp.int32)
    thr, exact = _kth_search(lambda t: count_where(lambda blk: blk >= t), k_eff)
    thr_sc[...] = jnp.broadcast_to(thr, thr_sc.shape)

    @pl.when(jnp.min(exact) == 0)
    def _():
        lane_f = lax.broadcasted_iota(jnp.int32, (1, kc), 1)
        for c in range(n_ch):
            s_sc[:, c * kc:(c + 1) * kc] = _rank(s_sc[:, c * kc:(c + 1) * kc], thr,
                                                 (c * kc + lane_f).astype(F32))
        snew_sc[...] = _rank(snew_sc[...], thr, (past + j_idx).astype(F32))
        cut = _tie_cut(lambda j: count_where(lambda blk: blk < j), k_eff, n_bits)
        for c in range(n_ch):
            s_sc[:, c * kc:(c + 1) * kc] = jnp.where(s_sc[:, c * kc:(c + 1) * kc] <= cut, 1.0, -1.0)
        snew_sc[...] = jnp.where(snew_sc[...] <= cut, 1.0, -1.0)
        thr_sc[...] = jnp.zeros_like(thr_sc)

    c2 = HEAD_DIM ** -0.5 * LOG2E
    thr_rows = jnp.tile(thr_sc[:, 0:1], (GROUP, 1))
    sel_new = jnp.tile(snew_sc[...], (GROUP, 1)) >= thr_rows
    m = []
    l = []
    acc = []
    for g in range(N_KV_HEADS):
        qg = q_ref[0, g]
        kn = pad_keys(kn_ref[0, :, g * HEAD_DIM:(g + 1) * HEAD_DIM])
        vn = pad_keys(vn_ref[0, :, g * HEAD_DIM:(g + 1) * HEAD_DIM])
        s = jnp.where(sel_new, _dot_nt(qg, kn), NEG_BIG)
        m_g = jnp.max(s, axis=1, keepdims=True)
        p = jnp.exp2((s - m_g) * c2)
        m.append(m_g)
        l.append(jnp.sum(p, axis=1, keepdims=True))
        acc.append(_dot(p.astype(BF16), vn))

    for c in range(n_ch):
        gidx = b * n_ch + c
        slot = gidx % 2
        if c + 1 < n_ch:
            start_kv(b, c + 1, 1 - slot)
        else:
            @pl.when(b + 1 < nb)
            def _():
                start_kv(b + 1, 0, 1 - slot)
        wait_kv(b, c, slot)
        sel = jnp.tile(s_sc[:, c * kc:(c + 1) * kc], (GROUP, 1)) >= thr_rows
        for g in range(N_KV_HEADS):
            kg = k_buf[slot, :, g * HEAD_DIM:(g + 1) * HEAD_DIM].astype(BF16)
            vg = v_buf[slot, :, g * HEAD_DIM:(g + 1) * HEAD_DIM].astype(BF16)
            s = jnp.where(sel, _dot_nt(q_ref[0, g], kg), NEG_BIG)
            m_new = jnp.maximum(m[g], jnp.max(s, axis=1, keepdims=True))
            alpha = jnp.exp2((m[g] - m_new) * c2)
            p = jnp.exp2((s - m_new) * c2)
            l[g] = alpha * l[g] + jnp.sum(p, axis=1, keepdims=True)
            acc[g] = alpha * acc[g] + _dot(p.astype(BF16), vg)
            m[g] = m_new

    for g in range(N_KV_HEADS):
        o_ref[0, g] = (acc[g] / l[g]).astype(BF16)


def _sample_attention(page_table, iq, iw, ikn, q, kn, vn, cik, ck, cv, *, cp, page_base):
    nb, n_pages = page_table.shape
    page = cik.shape[1]
    t_new = ikn.shape[1]
    past = n_pages * page
    n_sel = min(TOPK_MAX, (past + t_new) // 4)
    n_bits = max(1, (past + t_new - 1).bit_length())
    rows = GROUP * t_new
    d_kv = ck.shape[2]
    body = functools.partial(_sattn_body, n_pages=n_pages, page=page, cp=cp, t_new=t_new,
                             n_sel=n_sel, n_bits=n_bits, page_base=page_base)
    seq3 = lambda a: pl.BlockSpec((1,) + a.shape[1:], lambda i, pt: (i, 0, 0))
    seq4 = lambda a: pl.BlockSpec((1,) + a.shape[1:], lambda i, pt: (i, 0, 0, 0))
    any_spec = pl.BlockSpec(memory_space=pl.ANY)
    grid_spec = pltpu.PrefetchScalarGridSpec(
        num_scalar_prefetch=1, grid=(nb,),
        in_specs=[seq3(iq), seq3(iw), seq3(ikn), seq4(q), seq3(kn), seq3(vn),
                  any_spec, any_spec, any_spec],
        out_specs=pl.BlockSpec((1, N_KV_HEADS, rows, HEAD_DIM), lambda i, pt: (i, 0, 0, 0)),
        scratch_shapes=[pltpu.VMEM((2, past, IDX_DIM), F32),
                        pltpu.VMEM((2, cp * page, d_kv), F32),
                        pltpu.VMEM((2, cp * page, d_kv), F32),
                        pltpu.VMEM((t_new, past), F32),
                        pltpu.VMEM((t_new, LANES), F32),
                        pltpu.VMEM((t_new, LANES), F32),
                        pltpu.SemaphoreType.DMA((2,)),
                        pltpu.SemaphoreType.DMA((2,)),
                        pltpu.SemaphoreType.DMA((2,))],
    )
    return pl.pallas_call(
        body, grid_spec=grid_spec,
        out_shape=jax.ShapeDtypeStruct((nb, N_KV_HEADS, rows, HEAD_DIM), BF16),
        compiler_params=_params(1), name="sample_attention",
    )(page_table.reshape(-1), iq, iw, ikn, q, kn, vn, cik, ck, cv)


def _tail_body(x_ref, oa_ref, ol_ref, ga_ref, gb_ref, wua_ref, wul_ref, wo_ref, w1_ref, w2_ref,
               gpost_ref, gfpre_ref, gfpost_ref, y_ref):
    merged = (jax.nn.sigmoid(ga_ref[...]) * _dot(oa_ref[...], wua_ref[...])
              + jax.nn.sigmoid(gb_ref[...]) * _dot(ol_ref[...], wul_ref[...]))
    x = x_ref[...] + _rmsnorm(_dot(merged.astype(BF16), wo_ref[...]), gpost_ref[...])
    hf = _rmsnorm(x, gfpre_ref[...]).astype(BF16)
    f = jnp.square(jnp.maximum(_dot(hf, w1_ref[...]), 0.0)).astype(BF16)
    y_ref[...] = x + _rmsnorm(_dot(f, w2_ref[...]), gfpost_ref[...])


def _tail(x, oa, ol, ga, gb, wua, wul, wo, w1, w2, gpost, gfpre, gfpost, *, tm):
    n, d = x.shape
    row = lambda a: pl.BlockSpec((tm, a.shape[1]), lambda i: (i, 0))
    consts = (wua, wul, wo, w1, w2, gpost, gfpre, gfpost)
    return pl.pallas_call(
        _tail_body, grid=(n // tm,),
        in_specs=[row(x), row(oa), row(ol), row(ga), row(gb)] + [_const_spec(c.shape) for c in consts],
        out_specs=row(x), out_shape=jax.ShapeDtypeStruct((n, d), F32),
        compiler_params=_params(1), name="merge_ffn_tail",
    )(x, oa, ol, ga, gb, *consts)


def _row_tile(n, want):
    t = min(want, n)
    while n % t:
        t //= 2
    return t


def _head_pairs_t(a, n_blk, tq):
    heads, _, d = a.shape
    a = a.reshape(heads // 2, 2, n_blk, tq, d).transpose(2, 0, 4, 1, 3)
    return a.reshape(n_blk, heads // 2, d, 2 * tq)


def _layer(xp, xs, ck, cv, cik, page_base, h0, conv0, page_table, w):
    bp, sp, d_model = xp.shape
    nb, t_new, _ = xs.shape
    assert bp == 1, "prompt group is a single sequence"
    assert t_new == SUBLANES, "one sample sequence must fill one 8-row group"
    d_attn = N_HEADS * HEAD_DIM
    d_kv = N_KV_HEADS * HEAD_DIM
    d_lru = w["w_conv"].shape[1]
    dims = dict(d_attn=d_attn, d_kv=d_kv, d_lru=d_lru)

    w_in = w["w_in"]
    o_iq_end = d_attn + 2 * d_kv + IDX_HEADS * IDX_DIM
    o_iw_end = o_iq_end + IDX_HEADS
    o_ik_end = o_iw_end + IDX_DIM
    wa = w_in[:, :o_iq_end].astype(BF16)
    wi = jnp.concatenate([w_in[:, o_iw_end:o_ik_end], w_in[:, o_iq_end:o_iw_end],
                          jnp.zeros((d_model, LANES - IDX_DIM - IDX_HEADS), F32)], axis=1).astype(BF16)
    wl = w_in[:, o_ik_end:].astype(BF16)
    lru_consts = (w["w_conv"], w["b_conv"][None], w["w_rg"].astype(BF16), w["b_rg"][None],
                  w["w_ig"].astype(BF16), w["b_ig"][None], w["lru_lambda"][None])
    tail_consts = (w["w_up_attn"].astype(BF16), w["w_up_lru"].astype(BF16), w["w_out"].astype(BF16),
                   w["w_ff1"].astype(BF16), w["w_ff2"].astype(BF16),
                   w["g_mix_post"][None], w["g_ffn_pre"][None], w["g_ffn_post"][None])
    g_pre = w["g_mix_pre"][None]

    ns = nb * t_new
    xs2 = xs.reshape(ns, d_model)
    tms = _row_tile(ns, 256)
    q, k, v, kb, vb, iq, ikw, xl, xg, ga, gb = _in_projection(xs2, g_pre, wa, wi, wl, tm=tms, **dims)
    prev = jnp.pad(conv0, ((0, 0), (SUBLANES - (CONV_W - 1), 0), (0, 0))).reshape(ns, d_lru)
    h0x = jnp.broadcast_to(h0[:, None, :], (nb, t_new, d_lru)).reshape(ns, d_lru)
    o_lru, h_all = _lru_sample(xl, xg, prev, h0x, *lru_consts, tm=tms, seg=t_new)
    iq_s = iq.reshape(IDX_HEADS, nb, t_new, IDX_DIM).transpose(1, 0, 2, 3).reshape(nb, IDX_HEADS * t_new, IDX_DIM)
    iw_s = ikw[:, IDX_DIM:IDX_DIM + IDX_HEADS].reshape(nb, t_new, IDX_HEADS).transpose(0, 2, 1)
    iw_s = iw_s.reshape(nb, IDX_HEADS * t_new, 1)
    ikn = ikw[:, :IDX_DIM].reshape(nb, t_new, IDX_DIM)
    q_s = q.reshape(nb, t_new, N_KV_HEADS, GROUP, HEAD_DIM).transpose(0, 2, 3, 1, 4)
    q_s = q_s.reshape(nb, N_KV_HEADS, GROUP * t_new, HEAD_DIM)
    n_pages = page_table.shape[1]
    o_att = _sample_attention(page_table, iq_s, iw_s, ikn, q_s,
                              k.reshape(nb, t_new, d_kv), v.reshape(nb, t_new, d_kv), cik, ck, cv,
                              cp=math.gcd(n_pages, 16), page_base=page_base)
    o_att = o_att.reshape(nb, N_KV_HEADS, GROUP, t_new, HEAD_DIM).transpose(0, 3, 1, 2, 4).reshape(ns, d_attn)
    ys = _tail(xs2, o_att, o_lru, ga, gb, *tail_consts, tm=tms).reshape(nb, t_new, d_model)
    st_s = (k.reshape(nb, t_new, N_KV_HEADS, HEAD_DIM), v.reshape(nb, t_new, N_KV_HEADS, HEAD_DIM), ikn,
            h_all.reshape(nb, t_new, d_lru)[:, -1], xl.reshape(nb, t_new, d_lru)[:, t_new - (CONV_W - 1):])

    xp2 = xp.reshape(sp, d_model)
    tmp = _row_tile(sp, 256)
    q, k, v, kb, vb, iq, ikw, xl, xg, ga, gb = _in_projection(xp2, g_pre, wa, wi, wl, tm=tmp, **dims)
    o_lru, h_last, tail3 = _lru_prompt(xl, xg, *lru_consts, tm=tmp)
    ik = ikw[:, :IDX_DIM]
    tq = _row_tile(sp, LANES)
    q2 = _head_pairs_t(q.reshape(sp, N_HEADS, HEAD_DIM).transpose(1, 0, 2), sp // tq, tq)
    o_att = _prompt_attention(_head_pairs_t(iq, sp // tq, tq), ikw[:, IDX_DIM:IDX_DIM + IDX_HEADS].T, q2,
                              ik.astype(BF16), kb, vb.T, tq=tq, kc=_row_tile(sp // 2, 512))
    yp = _tail(xp2, o_att, o_lru, ga, gb, *tail_consts, tm=tmp).reshape(1, sp, d_model)
    st_p = (k.reshape(1, sp, N_KV_HEADS, HEAD_DIM), v.reshape(1, sp, N_KV_HEADS, HEAD_DIM),
            ik.reshape(1, sp, IDX_DIM), h_last[SUBLANES - 1:], tail3[None, SUBLANES - (CONV_W - 1):])
    return yp, ys, st_p, st_s


def kernel(x_prompt, x_sample, cache_k, cache_v, cache_idx_k, state_h, state_conv, page_table, w_in, w_conv, b_conv, w_rg, b_rg, w_ig, b_ig, lru_lambda, w_up_attn, w_up_lru, w_out, g_mix_pre, g_mix_post, g_ffn_pre, g_ffn_post, w_ff1, w_ff2):
    names = ("w_in", "w_conv", "b_conv", "w_rg", "b_rg", "w_ig", "b_ig", "lru_lambda", "w_up_attn",
             "w_up_lru", "w_out", "g_mix_pre", "g_mix_post", "g_ffn_pre", "g_ffn_post", "w_ff1", "w_ff2")
    stacked = (w_in, w_conv, b_conv, w_rg, b_rg, w_ig, b_ig, lru_lambda, w_up_attn, w_up_lru, w_out,
               g_mix_pre, g_mix_post, g_ffn_pre, g_ffn_post, w_ff1, w_ff2)
    depth, n_pool, page = cache_idx_k.shape[:3]
    ck = cache_k.reshape(depth * n_pool, page, N_KV_HEADS * HEAD_DIM)
    cv = cache_v.reshape(depth * n_pool, page, N_KV_HEADS * HEAD_DIM)
    cik = cache_idx_k.reshape(depth * n_pool, page, IDX_DIM)
    yp, ys = x_prompt, x_sample
    new_p, new_s = [], []
    for layer in range(depth):
        w = {n: a[layer] for n, a in zip(names, stacked)}
        yp, ys, st_p, st_s = _layer(yp, ys, ck, cv, cik, layer * n_pool, state_h[layer], state_conv[layer],
                                    page_table, w)
        new_p.append(st_p)
        new_s.append(st_s)
    stack = lambda states, j: jnp.stack([s[j] for s in states], axis=0)
    return (yp, ys) + tuple(stack(new_p, j) for j in range(5)) + tuple(stack(new_s, j) for j in range(5))
```

```python
import functools
import math

import jax
import jax.numpy as jnp
from jax import lax
from jax.experimental import pallas as pl
from jax.experimental.pallas import tpu as pltpu

N_HEADS = 8
HEAD_DIM = 128
N_KV_HEADS = 2
GROUP = N_HEADS // N_KV_HEADS
IDX_HEADS = 8
IDX_DIM = 64
TOPK_MAX = 256
LRU_BLOCKS = 8
CONV_W = 4
LRU_C = 8.0
EPS = 1e-6

LANES = 128
SUBLANES = 8
VMEM_LIMIT = 56 * 1024 * 1024

F32 = jnp.float32
BF16 = jnp.bfloat16
NEG_BIG = -1e30
IDX_BIG = 1e9
LOG2E = math.log2(math.e)


def _const_spec(shape):
    nd = len(shape)
    return pl.BlockSpec(shape, lambda *_: (0,) * nd, pipeline_mode=pl.Buffered(1))


def _params(n_grid):
    return pltpu.CompilerParams(dimension_semantics=("arbitrary",) * n_grid,
                                vmem_limit_bytes=VMEM_LIMIT)


def _rmsnorm(x, g):
    return x * lax.rsqrt(jnp.mean(x * x, axis=-1, keepdims=True) + EPS) * g


def _dot(a, b):
    return jnp.dot(a, b, preferred_element_type=F32)


def _dot_nt(a, b):
    return lax.dot_general(a, b, (((1,), (1,)), ((), ())), preferred_element_type=F32)


def _inproj_body(x_ref, g_ref, wa_ref, wi_ref, wl_ref,
                 q_ref, k_ref, v_ref, kb_ref, vb_ref, iq_ref, ikw_ref, xl_ref, xg_ref, ga_ref, gb_ref,
                 *, d_attn, d_kv, d_lru, d_model):
    hn = _rmsnorm(x_ref[...], g_ref[...]).astype(BF16)
    q_ref[...] = _dot(hn, wa_ref[:, :d_attn]).astype(BF16)
    k = _dot(hn, wa_ref[:, d_attn:d_attn + d_kv])
    k_ref[...] = k
    kb_ref[...] = k.astype(BF16)
    v = _dot(hn, wa_ref[:, d_attn + d_kv:d_attn + 2 * d_kv])
    v_ref[...] = v
    vb_ref[...] = v.astype(BF16)
    iq = _dot(hn, wa_ref[:, d_attn + 2 * d_kv:])
    for h in range(IDX_HEADS):
        iq_ref[h] = iq[:, h * IDX_DIM:(h + 1) * IDX_DIM].astype(BF16)
    ikw_ref[...] = _dot(hn, wi_ref[...])
    xl_ref[...] = _dot(hn, wl_ref[:, :d_lru])
    xg_ref[...] = _dot(hn, wl_ref[:, d_lru:2 * d_lru])
    ga_ref[...] = _dot(hn, wl_ref[:, 2 * d_lru:2 * d_lru + d_model])
    gb_ref[...] = _dot(hn, wl_ref[:, 2 * d_lru + d_model:])


def _in_projection(x, g, wa, wi, wl, *, d_attn, d_kv, d_lru, tm):
    n, d_model = x.shape
    grid = (n // tm,)
    row = lambda w: pl.BlockSpec((tm, w), lambda i: (i, 0))
    out_shape = (
        jax.ShapeDtypeStruct((n, d_attn), BF16),
        jax.ShapeDtypeStruct((n, d_kv), F32),
        jax.ShapeDtypeStruct((n, d_kv), F32),
        jax.ShapeDtypeStruct((n, d_kv), BF16),
        jax.ShapeDtypeStruct((n, d_kv), BF16),
        jax.ShapeDtypeStruct((IDX_HEADS, n, IDX_DIM), BF16),
        jax.ShapeDtypeStruct((n, LANES), F32),
        jax.ShapeDtypeStruct((n, d_lru), F32),
        jax.ShapeDtypeStruct((n, d_lru), F32),
        jax.ShapeDtypeStruct((n, d_model), F32),
        jax.ShapeDtypeStruct((n, d_model), F32),
    )
    out_specs = (row(d_attn), row(d_kv), row(d_kv), row(d_kv), row(d_kv),
                 pl.BlockSpec((IDX_HEADS, tm, IDX_DIM), lambda i: (0, i, 0)),
                 row(LANES), row(d_lru), row(d_lru), row(d_model), row(d_model))
    body = functools.partial(_inproj_body, d_attn=d_attn, d_kv=d_kv, d_lru=d_lru, d_model=d_model)
    return pl.pallas_call(
        body, grid=grid,
        in_specs=[row(d_model), _const_spec(g.shape), _const_spec(wa.shape), _const_spec(wi.shape),
                  _const_spec(wl.shape)],
        out_specs=out_specs, out_shape=out_shape, compiler_params=_params(1), name="in_projection",
    )(x, g, wa, wi, wl)


def _shift_rows(cur, prev, j, row_in_seg):
    tm = cur.shape[0]
    from_prev = pltpu.roll(prev, tm - SUBLANES + j, axis=0)
    return jnp.where(row_in_seg < j, from_prev, pltpu.roll(cur, j, axis=0))


def _lru_core(xl, xg, prev, h_in, wc, bc, wrg, brg, wig, big, lam, *, seg):
    tm, d = xl.shape
    row = lax.broadcasted_iota(jnp.int32, (tm, 1), 0)
    row_in_seg = row % seg
    u = bc + wc[CONV_W - 1:CONV_W] * xl
    for j in range(1, CONV_W):
        u = u + wc[CONV_W - 1 - j:CONV_W - j] * _shift_rows(xl, prev, j, row_in_seg)
    ub = u.astype(BF16)
    blk = d // LRU_BLOCKS
    r = jnp.concatenate([_dot(ub[:, n * blk:(n + 1) * blk], wrg[n]) for n in range(LRU_BLOCKS)], axis=1)
    ig = jnp.concatenate([_dot(ub[:, n * blk:(n + 1) * blk], wig[n]) for n in range(LRU_BLOCKS)], axis=1)
    r = jax.nn.sigmoid(r + brg)
    ig = jax.nn.sigmoid(ig + big)
    softplus_neg = jnp.maximum(-lam, 0.0) + jnp.log1p(jnp.exp(-jnp.abs(lam)))
    log_a = -LRU_C * r * softplus_neg
    a = jnp.exp(log_a)
    b = jnp.sqrt(-jnp.tanh(log_a) * (a * a + 1.0)) * (ig * u)
    d_step = 1
    while d_step < seg:
        keep = row_in_seg >= d_step
        a_sh = jnp.where(keep, pltpu.roll(a, d_step, axis=0), 1.0)
        b_sh = jnp.where(keep, pltpu.roll(b, d_step, axis=0), 0.0)
        b = a * b_sh + b
        a = a * a_sh
        d_step *= 2
    h = a * h_in + b
    gelu = 0.5 * xg * (1.0 + jnp.tanh(math.sqrt(2.0 / math.pi) * (xg + 0.044715 * (xg * xg * xg))))
    return h * gelu, h


def _lru_prompt_body(xl_ref, xg_ref, wc_ref, bc_ref, wrg_ref, brg_ref, wig_ref, big_ref, lam_ref,
                     o_ref, h_ref, tail_ref, hc_sc, prev_sc):
    @pl.when(pl.program_id(0) == 0)
    def _():
        hc_sc[...] = jnp.zeros_like(hc_sc)
        prev_sc[...] = jnp.zeros_like(prev_sc)

    xl = xl_ref[...]
    tm = xl.shape[0]
    prev = jnp.tile(prev_sc[...], (tm // SUBLANES, 1))
    out, h = _lru_core(xl, xg_ref[...], prev, hc_sc[0:1, :], wc_ref[...], bc_ref[...], wrg_ref[...],
                       brg_ref[...], wig_ref[...], big_ref[...], lam_ref[...], seg=tm)
    o_ref[...] = out.astype(BF16)
    last = h[tm - SUBLANES:, :]
    hc_sc[...] = jnp.broadcast_to(last[SUBLANES - 1:SUBLANES, :], hc_sc.shape)
    h_ref[...] = last
    prev_sc[...] = xl[tm - SUBLANES:, :]
    tail_ref[...] = xl[tm - SUBLANES:, :]


def _lru_prompt(xl, xg, wc, bc, wrg, brg, wig, big, lam, *, tm):
    n, d = xl.shape
    row = pl.BlockSpec((tm, d), lambda i: (i, 0))
    last8 = pl.BlockSpec((SUBLANES, d), lambda i: (0, 0))
    consts = (wc, bc, wrg, brg, wig, big, lam)
    return pl.pallas_call(
        _lru_prompt_body, grid=(n // tm,),
        in_specs=[row, row] + [_const_spec(c.shape) for c in consts],
        out_specs=(row, last8, last8),
        out_shape=(jax.ShapeDtypeStruct((n, d), BF16),
                   jax.ShapeDtypeStruct((SUBLANES, d), F32),
                   jax.ShapeDtypeStruct((SUBLANES, d), F32)),
        scratch_shapes=[pltpu.VMEM((SUBLANES, d), F32), pltpu.VMEM((SUBLANES, d), F32)],
        compiler_params=_params(1), name="lru_prompt",
    )(xl, xg, *consts)


def _lru_sample_body(xl_ref, xg_ref, prev_ref, h0_ref, wc_ref, bc_ref, wrg_ref, brg_ref, wig_ref,
                     big_ref, lam_ref, o_ref, h_ref, *, seg):
    out, h = _lru_core(xl_ref[...], xg_ref[...], prev_ref[...], h0_ref[...], wc_ref[...], bc_ref[...],
                       wrg_ref[...], brg_ref[...], wig_ref[...], big_ref[...], lam_ref[...], seg=seg)
    o_ref[...] = out.astype(BF16)
    h_ref[...] = h


def _lru_sample(xl, xg, prev, h0, wc, bc, wrg, brg, wig, big, lam, *, tm, seg):
    n, d = xl.shape
    row = pl.BlockSpec((tm, d), lambda i: (i, 0))
    consts = (wc, bc, wrg, brg, wig, big, lam)
    return pl.pallas_call(
        functools.partial(_lru_sample_body, seg=seg), grid=(n // tm,),
        in_specs=[row, row, row, row] + [_const_spec(c.shape) for c in consts],
        out_specs=(row, row),
        out_shape=(jax.ShapeDtypeStruct((n, d), BF16), jax.ShapeDtypeStruct((n, d), F32)),
        compiler_params=_params(1), name="lru_sample",
    )(xl, xg, prev, h0, *consts)


def _ordered_bits_to_float(u):
    key = u ^ jnp.int32(-2 ** 31)
    bits = jnp.where(key >= 0, key, key ^ jnp.int32(0x7FFFFFFF))
    return lax.bitcast_convert_type(bits, F32)


REFINE_STEPS = 12


def _float_to_ordered_bits(x):
    bits = lax.bitcast_convert_type(x, jnp.int32)
    key = jnp.where(bits >= 0, bits, bits ^ jnp.int32(0x7FFFFFFF))
    return key ^ jnp.int32(-2 ** 31)


def _kth_search(count_ge, k_eff, bracket=None):
    shape = k_eff.shape
    zero = jnp.zeros(shape, jnp.int32)
    if bracket is None:
        i0, u0 = jnp.int32(0), zero
    else:
        u_lo, u_hi = (_float_to_ordered_bits(x) for x in bracket)
        i0 = jnp.minimum(jnp.min(lax.clz(u_lo ^ u_hi)), 31)
        keep = lax.shift_left(lax.shift_left(jnp.int32(-1), 31 - i0), 1)
        u0 = u_hi & keep

    def cond(st):
        return (st[0] < 32) & (st[4] == 0)

    def body(st):
        i, u, thr, exact, _ = st
        cand = u | lax.shift_left(jnp.int32(1), 31 - i)
        t = _ordered_bits_to_float(cand)
        c = count_ge(t)
        hit = (c == k_eff) & (exact == 0)
        thr = jnp.where(hit, t, thr)
        exact = jnp.where(hit, 1, exact)
        u = jnp.where(c >= k_eff, cand, u)
        return i + 1, u, thr, exact, jnp.min(exact)

    init = (i0, u0, jnp.zeros(shape, F32), zero, jnp.int32(0))
    _, u, thr, exact, all_exact = lax.while_loop(cond, body, init)

    def refine_cond(st):
        return (st[0] < REFINE_STEPS) & (st[5] == 0)

    def refine(st):
        j, lo, hi, thr, exact, _ = st
        mid = lo + (hi - lo) * 0.5
        c = count_ge(mid)
        hit = (c == k_eff) & (exact == 0)
        thr = jnp.where(hit, mid, thr)
        exact = jnp.where(hit, 1, exact)
        return j + 1, jnp.where(c > k_eff, mid, lo), jnp.where(c < k_eff, mid, hi), thr, exact, jnp.min(exact)

    init = (jnp.int32(0), _ordered_bits_to_float(u), _ordered_bits_to_float(u + 1), thr, exact, all_exact)
    _, lo, _, thr, exact, _ = lax.while_loop(refine_cond, refine, init)
    return jnp.where(exact == 1, thr, lo), exact


def _tie_cut(count_lt, k_eff, n_bits):
    def step(i, j):
        cand = j | lax.shift_left(jnp.int32(1), n_bits - 1 - i)
        ok = count_lt(cand.astype(F32)) < k_eff
        return jnp.where(ok, cand, j)
    j = lax.fori_loop(0, n_bits, step, jnp.zeros(k_eff.shape, jnp.int32))
    return j.astype(F32)


def _rank(score, thr, key_idx):
    return jnp.where(score > thr, -1.0, jnp.where(score == thr, key_idx, IDX_BIG))


def _tree_sum(terms):
    while len(terms) > 1:
        terms = [a + b for a, b in zip(terms[0::2], terms[1::2])] + ([terms[-1]] if len(terms) % 2 else [])
    return terms[0]


COUNT_ROWS = 64


def _pattn_body(iq2_ref, iwt_ref, q2_ref, ik_ref, k_ref, vt_ref, o_ref,
                s_sc, cm_sc, sa_sc, sb_sc, pa_sc, pb_sc, thr_sc, m_sc, l_sc, acc_sc,
                *, tq, kc, n_sel, n_bits):
    qb = pl.program_id(0)
    n_chunks = ((qb + 1) * tq + kc - 1) // kc
    n_chunks_even = (n_chunks + 1) // 2 * 2
    q_pos = qb * tq + lax.broadcasted_iota(jnp.int32, (1, tq), 1)
    n_pairs = N_HEADS // 2
    iw = iwt_ref[...] * (IDX_DIM ** -0.5 * IDX_HEADS ** -0.5)

    def chunk_off(c):
        return pl.multiple_of(c * kc, kc)

    def score_chunk(c, carry):
        off = chunk_off(c)
        ikc = ik_ref[pl.ds(off, kc), :]
        acc = jnp.zeros((kc, tq), F32)
        for hp in range(IDX_HEADS // 2):
            lg = _dot(ikc, iq2_ref[0, hp])
            acc = acc + jnp.maximum(lg[:, :tq], 0.0) * iw[2 * hp:2 * hp + 1, :]
            acc = acc + jnp.maximum(lg[:, tq:], 0.0) * iw[2 * hp + 1:2 * hp + 2, :]
        k_pos = off + lax.broadcasted_iota(jnp.int32, (kc, 1), 0)
        sc = jnp.where(k_pos <= q_pos, acc, -jnp.inf)
        s_sc[pl.ds(off, kc), :] = sc
        cm_sc[...] = functools.reduce(jnp.maximum, [cm_sc[...]] + [
            sc[j * TOPK_MAX:(j + 1) * TOPK_MAX] for j in range(kc // TOPK_MAX)])
        return carry
    cm_sc[...] = jnp.full_like(cm_sc, -jnp.inf)
    lax.fori_loop(0, n_chunks_even, score_chunk, 0)
    bracket = (jnp.min(cm_sc[...], axis=0, keepdims=True), jnp.max(cm_sc[...], axis=0, keepdims=True))

    def count_where(pred):
        def body(c, acc):
            off = chunk_off(c)
            for j in range(kc // COUNT_ROWS):
                acc = acc + jnp.where(pred(s_sc[pl.ds(off + j * COUNT_ROWS, COUNT_ROWS), :]), 1, 0)
            return acc
        acc = lax.fori_loop(0, n_chunks, body, jnp.zeros((COUNT_ROWS, tq), jnp.int32))
        return jnp.sum(acc, axis=0, keepdims=True)

    k_eff = jnp.minimum(q_pos + 1, n_sel)
    thr, exact = _kth_search(lambda t: count_where(lambda blk: blk >= t), k_eff, bracket)
    thr_sc[...] = jnp.broadcast_to(thr, thr_sc.shape)

    @pl.when(jnp.min(exact) == 0)
    def _():
        def rank_chunk(c, carry):
            off = chunk_off(c)
            key_idx = (off + lax.broadcasted_iota(jnp.int32, (kc, 1), 0)).astype(F32)
            s_sc[pl.ds(off, kc), :] = _rank(s_sc[pl.ds(off, kc), :], thr, key_idx)
            return carry
        lax.fori_loop(0, n_chunks, rank_chunk, 0)
        cut = _tie_cut(lambda j: count_where(lambda blk: blk < j), k_eff, n_bits)

        def recode_chunk(c, carry):
            off = chunk_off(c)
            s_sc[pl.ds(off, kc), :] = jnp.where(s_sc[pl.ds(off, kc), :] <= cut, 1.0, -1.0)
            return carry
        lax.fori_loop(0, n_chunks, recode_chunk, 0)
        thr_sc[...] = jnp.zeros_like(thr_sc)

    sel_thr = thr_sc[0:1, :]
    m_sc[...] = jnp.full_like(m_sc, NEG_BIG)
    l_sc[...] = jnp.zeros_like(l_sc)
    acc_sc[...] = jnp.zeros_like(acc_sc)
    c2 = HEAD_DIM ** -0.5 * LOG2E
    last_chunk = k_ref.shape[0] // kc - 1

    def qk(c, s_buf):
        off = chunk_off(jnp.minimum(c, last_chunk))
        for hp in range(n_pairs):
            g = (2 * hp) // GROUP
            s_buf[:, 2 * hp * tq:(2 * hp + 2) * tq] = _dot(
                k_ref[pl.ds(off, kc), g * HEAD_DIM:(g + 1) * HEAD_DIM], q2_ref[0, hp])

    def pv(c, p_buf, alphas):
        off = chunk_off(jnp.maximum(c, 0))
        for hp in range(n_pairs):
            g = (2 * hp) // GROUP
            acc_sc[hp] = alphas[hp] * acc_sc[hp] + _dot(
                vt_ref[g * HEAD_DIM:(g + 1) * HEAD_DIM, pl.ds(off, kc)],
                p_buf[:, 2 * hp * tq:(2 * hp + 2) * tq])

    def softmax(c, s_buf, p_buf):
        bias = jnp.where(s_sc[pl.ds(chunk_off(c), kc), :] >= sel_thr, 0.0, NEG_BIG)
        alphas = []
        for h in range(N_HEADS):
            s = s_buf[:, h * tq:(h + 1) * tq] + bias
            s_buf[:, h * tq:(h + 1) * tq] = s
            m_old = m_sc[h]
            m_new = jnp.maximum(m_old, jnp.max(s, axis=0, keepdims=True))
            alpha = jnp.exp2((m_old - m_new) * c2)
            p = jnp.exp2((s_buf[:, h * tq:(h + 1) * tq] - m_new[0:1, :]) * c2)
            l_sc[h] = alpha * l_sc[h] + jnp.sum(p, axis=0, keepdims=True)
            m_sc[h] = m_new
            p_buf[:, h * tq:(h + 1) * tq] = p.astype(BF16)
            alphas.append(alpha[0:1, :])
        return [jnp.concatenate(alphas[2 * hp:2 * hp + 2], axis=1) for hp in range(n_pairs)]

    def stage(c, s_cur, p_cur, s_next, p_prev, alphas_prev):
        qk(c + 1, s_next)
        pv(c - 1, p_prev, alphas_prev)
        return softmax(c, s_cur, p_cur)

    qk(0, sa_sc)
    pb_sc[...] = jnp.zeros_like(pb_sc)

    def attn_two_chunks(i, alphas):
        alphas = stage(2 * i, sa_sc, pa_sc, sb_sc, pb_sc, alphas)
        return stage(2 * i + 1, sb_sc, pb_sc, sa_sc, pa_sc, alphas)
    ones = [jnp.ones((1, 2 * tq), F32)] * n_pairs
    alphas = lax.fori_loop(0, n_chunks_even // 2, attn_two_chunks, ones)
    pv(n_chunks_even - 1, pb_sc, alphas)

    for h in range(N_HEADS):
        o_t = acc_sc[h // 2][:, (h % 2) * tq:(h % 2 + 1) * tq] / l_sc[h][0:1, :]
        o_ref[:, h * HEAD_DIM:(h + 1) * HEAD_DIM] = o_t.T.astype(BF16)


def _prompt_attention(iq2, iwt, q2, ik, kb, vt, *, tq, kc):
    n = kb.shape[0]
    d_attn = N_HEADS * HEAD_DIM
    n_sel = min(TOPK_MAX, n // 4)
    n_bits = max(1, (n - 1).bit_length())
    assert (n // kc) % 2 == 0, "the attention pipeline consumes key chunks in pairs"
    assert kc % TOPK_MAX == 0, "a key chunk holds whole sets of residue classes"
    body = functools.partial(_pattn_body, tq=tq, kc=kc, n_sel=n_sel, n_bits=n_bits)
    blk4 = lambda a: pl.BlockSpec((1,) + a.shape[1:], lambda i: (i, 0, 0, 0))
    return pl.pallas_call(
        body, grid=(n // tq,),
        in_specs=[blk4(iq2), pl.BlockSpec((IDX_HEADS, tq), lambda i: (0, i)), blk4(q2),
                  _const_spec(ik.shape), _const_spec(kb.shape), _const_spec(vt.shape)],
        out_specs=pl.BlockSpec((tq, d_attn), lambda i: (i, 0)),
        out_shape=jax.ShapeDtypeStruct((n, d_attn), BF16),
        scratch_shapes=[pltpu.VMEM((n, tq), F32),
                        pltpu.VMEM((TOPK_MAX, tq), F32),
                        pltpu.VMEM((kc, N_HEADS * tq), F32),
                        pltpu.VMEM((kc, N_HEADS * tq), F32),
                        pltpu.VMEM((kc, N_HEADS * tq), BF16),
                        pltpu.VMEM((kc, N_HEADS * tq), BF16),
                        pltpu.VMEM((SUBLANES, tq), F32),
                        pltpu.VMEM((N_HEADS, SUBLANES, tq), F32),
                        pltpu.VMEM((N_HEADS, SUBLANES, tq), F32),
                        pltpu.VMEM((N_HEADS // 2, HEAD_DIM, 2 * tq), F32)],
        compiler_params=_params(1), name="prompt_attention",
    )(iq2, iwt, q2, ik, kb, vt)


def _ssel_body(pt_ref, iq_ref, iw_ref, ikn_ref, cik_hbm, o_ref,
               ik_buf, s_sc, snew_sc, cm_sc, thr_sc, ik_sem,
               *, n_pages, page, cp, t_new, n_sel, n_bits, page_base, n_seq):
    grp = pl.program_id(0)
    nb = pl.num_programs(0) * n_seq
    past = n_pages * page
    n_ch = n_pages // cp
    kc = cp * page
    rows = n_seq * t_new

    def start_ik(seq, slot):
        def body(p, c):
            pltpu.make_async_copy(cik_hbm.at[pt_ref[seq * n_pages + p] + page_base],
                                  ik_buf.at[slot, :, pl.ds(p * page, page)], ik_sem.at[slot]).start()
            return c
        lax.fori_loop(0, n_pages, body, 0)

    def wait_ik(slot):
        pltpu.make_async_copy(ik_buf.at[slot], ik_buf.at[slot], ik_sem.at[slot]).wait()

    @pl.when(grp == 0)
    def _():
        start_ik(0, 0)

    t_idx = lax.broadcasted_iota(jnp.int32, (t_new, LANES), 0)
    j_idx = lax.broadcasted_iota(jnp.int32, (t_new, LANES), 1)

    def head_sum(x):
        return _tree_sum([x[h * t_new:(h + 1) * t_new] for h in range(IDX_HEADS)])

    def score_seq(i, carry):
        seq = grp * n_seq + i
        slot = seq % 2

        @pl.when(seq + 1 < nb)
        def _():
            start_ik(seq + 1, 1 - slot)

        wait_ik(slot)
        iq = iq_ref[i]
        iw = iw_ref[i] * (IDX_DIM ** -0.5 * IDX_HEADS ** -0.5)
        r0 = pl.multiple_of(i * t_new, t_new)
        cm = [jnp.full((t_new, LANES), -jnp.inf, F32)] * 2
        for c in range(n_ch):
            ikc = ik_buf[slot, :, c * kc:(c + 1) * kc].astype(BF16)
            sc = head_sum(jnp.maximum(_dot(iq, ikc), 0.0) * iw)
            s_sc[pl.ds(r0, t_new), c * kc:(c + 1) * kc] = sc
            for j in range(kc // LANES):
                cm[j % 2] = jnp.maximum(cm[j % 2], sc[:, j * LANES:(j + 1) * LANES])
        cm_sc[pl.ds(r0, t_new), :] = jnp.concatenate(cm, axis=1)
        ikn = jnp.concatenate([ikn_ref[i], jnp.zeros((LANES - t_new, IDX_DIM), F32)], axis=0)
        s_new = head_sum(jnp.maximum(_dot_nt(iq, ikn.astype(BF16)), 0.0) * iw)
        snew_sc[pl.ds(r0, t_new), :] = jnp.where(j_idx <= t_idx, s_new, -jnp.inf)
        return carry
    lax.fori_loop(0, n_seq, score_seq, 0)

    def count_where(pred):
        terms = [jnp.where(pred(snew_sc[...]), 1, 0)]
        for j in range(past // LANES):
            terms.append(jnp.where(pred(s_sc[:, j * LANES:(j + 1) * LANES]), 1, 0))
        return jnp.sum(_tree_sum(terms), axis=1, keepdims=True)

    k_eff = jnp.full((rows, 1), n_sel, jnp.int32)
    assert n_sel <= 2 * LANES <= past
    lo = jnp.min(cm_sc[...], axis=1, keepdims=True)
    hi = jnp.maximum(jnp.max(cm_sc[...], axis=1, keepdims=True), jnp.max(snew_sc[...], axis=1, keepdims=True))
    thr, exact = _kth_search(lambda t: count_where(lambda blk: blk >= t), k_eff, (lo, hi))
    thr_sc[...] = jnp.broadcast_to(thr, thr_sc.shape)

    @pl.when(jnp.min(exact) == 0)
    def _():
        lane_f = lax.broadcasted_iota(jnp.int32, (1, kc), 1)
        j_all = lax.broadcasted_iota(jnp.int32, (1, LANES), 1)
        for c in range(n_ch):
            s_sc[:, c * kc:(c + 1) * kc] = _rank(s_sc[:, c * kc:(c + 1) * kc], thr,
                                                 (c * kc + lane_f).astype(F32))
        snew_sc[...] = _rank(snew_sc[...], thr, (past + j_all).astype(F32))
        cut = _tie_cut(lambda j: count_where(lambda blk: blk < j), k_eff, n_bits)
        for c in range(n_ch):
            s_sc[:, c * kc:(c + 1) * kc] = jnp.where(s_sc[:, c * kc:(c + 1) * kc] <= cut, 1.0, -1.0)
        snew_sc[...] = jnp.where(snew_sc[...] <= cut, 1.0, -1.0)
        thr_sc[...] = jnp.zeros_like(thr_sc)

    sel_thr = thr_sc[:, 0:1]
    for c in range(n_ch):
        o_ref[:, c * kc:(c + 1) * kc] = jnp.where(s_sc[:, c * kc:(c + 1) * kc] >= sel_thr, 0.0, NEG_BIG)
    o_ref[:, past:] = jnp.where(snew_sc[...] >= sel_thr, 0.0, NEG_BIG)


def _sample_select(page_table, iq, iw, ikn, cik, *, cp, page_base, n_seq):
    nb, n_pages = page_table.shape
    page = cik.shape[2]
    t_new = ikn.shape[1]
    past = n_pages * page
    n_sel = min(TOPK_MAX, (past + t_new) // 4)
    n_bits = max(1, (past + t_new - 1).bit_length())
    rows = n_seq * t_new
    body = functools.partial(_ssel_body, n_pages=n_pages, page=page, cp=cp, t_new=t_new,
                             n_sel=n_sel, n_bits=n_bits, page_base=page_base, n_seq=n_seq)
    grp3 = lambda a: pl.BlockSpec((n_seq,) + a.shape[1:], lambda i, pt: (i, 0, 0))
    grid_spec = pltpu.PrefetchScalarGridSpec(
        num_scalar_prefetch=1, grid=(nb // n_seq,),
        in_specs=[grp3(iq), grp3(iw), grp3(ikn), pl.BlockSpec(memory_space=pl.ANY)],
        out_specs=pl.BlockSpec((rows, past + LANES), lambda i, pt: (i, 0)),
        scratch_shapes=[pltpu.VMEM((2, IDX_DIM, past), F32),
                        pltpu.VMEM((rows, past), F32),
                        pltpu.VMEM((rows, LANES), F32),
                        pltpu.VMEM((rows, 2 * LANES), F32),
                        pltpu.VMEM((rows, LANES), F32),
                        pltpu.SemaphoreType.DMA((2,))],
    )
    return pl.pallas_call(
        body, grid_spec=grid_spec,
        out_shape=jax.ShapeDtypeStruct((nb * t_new, past + LANES), F32),
        compiler_params=_params(1), name="sample_select",
    )(page_table.reshape(-1), iq, iw, ikn, cik)


def _sattn_body(pt_ref, bias_ref, q_ref, kn_ref, vn_ref, ck_hbm, cv_hbm,
                o_ref, k_buf, v_buf, k_sem, v_sem, *, n_pages, page, cp, t_new, page_base):
    b = pl.program_id(0)
    nb = pl.num_programs(0)
    past = n_pages * page
    n_ch = n_pages // cp
    kc = cp * page
    page_rows = page * N_KV_HEADS

    def start_kv(seq, chunk, slot):
        def body(p, c):
            src = pl.ds((pt_ref[seq * n_pages + chunk * cp + p] + page_base) * page_rows, page_rows)
            dst = pl.ds(p * page_rows, page_rows)
            pltpu.make_async_copy(ck_hbm.at[src, :], k_buf.at[slot, dst, :], k_sem.at[slot]).start()
            pltpu.make_async_copy(cv_hbm.at[src, :], v_buf.at[slot, dst, :], v_sem.at[slot]).start()
            return c
        lax.fori_loop(0, cp, body, 0)

    def wait_kv(slot):
        pltpu.make_async_copy(k_buf.at[slot], k_buf.at[slot], k_sem.at[slot]).wait()
        pltpu.make_async_copy(v_buf.at[slot], v_buf.at[slot], v_sem.at[slot]).wait()

    @pl.when(b == 0)
    def _():
        start_kv(0, 0, 0)

    def pad_keys(x):
        return jnp.concatenate([x, jnp.zeros((LANES - t_new, x.shape[1]), F32)], axis=0).astype(BF16)

    c2 = HEAD_DIM ** -0.5 * LOG2E
    bias_new = jnp.tile(bias_ref[:, past:], (GROUP, 1))
    m = []
    l = []
    acc = []
    for g in range(N_KV_HEADS):
        kn = pad_keys(kn_ref[0, :, g * HEAD_DIM:(g + 1) * HEAD_DIM])
        vn = pad_keys(vn_ref[0, :, g * HEAD_DIM:(g + 1) * HEAD_DIM])
        s = _dot_nt(q_ref[0, g], kn) + bias_new
        m_g = jnp.max(s, axis=1, keepdims=True)
        p = jnp.exp2((s - m_g) * c2)
        m.append(m_g)
        l.append(jnp.sum(p, axis=1, keepdims=True))
        acc.append(_dot(p.astype(BF16), vn))

    for c in range(n_ch):
        slot = (b * n_ch + c) % 2
        if c + 1 < n_ch:
            start_kv(b, c + 1, 1 - slot)
        else:
            @pl.when(b + 1 < nb)
            def _():
                start_kv(b + 1, 0, 1 - slot)
        wait_kv(slot)
        bias = jnp.tile(bias_ref[:, c * kc:(c + 1) * kc], (GROUP, 1))
        for g in range(N_KV_HEADS):
            kg = k_buf[slot, pl.ds(g, kc, stride=N_KV_HEADS), :].astype(BF16)
            vg = v_buf[slot, pl.ds(g, kc, stride=N_KV_HEADS), :].astype(BF16)
            s = _dot_nt(q_ref[0, g], kg) + bias
            m_new = jnp.maximum(m[g], jnp.max(s, axis=1, keepdims=True))
            alpha = jnp.exp2((m[g] - m_new) * c2)
            p = jnp.exp2((s - m_new) * c2)
            l[g] = alpha * l[g] + jnp.sum(p, axis=1, keepdims=True)
            acc[g] = alpha * acc[g] + _dot(p.astype(BF16), vg)
            m[g] = m_new

    for g in range(N_KV_HEADS):
        o_ref[0, g] = (acc[g] / l[g]).astype(BF16)


def _sample_attention(page_table, bias, q, kn, vn, ck, cv, *, page, cp, page_base):
    nb, n_pages = page_table.shape
    t_new = kn.shape[1]
    rows = GROUP * t_new
    body = functools.partial(_sattn_body, n_pages=n_pages, page=page, cp=cp, t_new=t_new,
                             page_base=page_base)
    seq3 = lambda a: pl.BlockSpec((1,) + a.shape[1:], lambda i, pt: (i, 0, 0))
    seq4 = lambda a: pl.BlockSpec((1,) + a.shape[1:], lambda i, pt: (i, 0, 0, 0))
    any_spec = pl.BlockSpec(memory_space=pl.ANY)
    grid_spec = pltpu.PrefetchScalarGridSpec(
        num_scalar_prefetch=1, grid=(nb,),
        in_specs=[pl.BlockSpec((t_new, bias.shape[1]), lambda i, pt: (i, 0)), seq4(q), seq3(kn), seq3(vn),
                  any_spec, any_spec],
        out_specs=pl.BlockSpec((1, N_KV_HEADS, rows, HEAD_DIM), lambda i, pt: (i, 0, 0, 0)),
        scratch_shapes=[pltpu.VMEM((2, cp * page * N_KV_HEADS, HEAD_DIM), F32),
                        pltpu.VMEM((2, cp * page * N_KV_HEADS, HEAD_DIM), F32),
                        pltpu.SemaphoreType.DMA((2,)),
                        pltpu.SemaphoreType.DMA((2,))],
    )
    return pl.pallas_call(
        body, grid_spec=grid_spec,
        out_shape=jax.ShapeDtypeStruct((nb, N_KV_HEADS, rows, HEAD_DIM), BF16),
        compiler_params=_params(1), name="sample_attention",
    )(page_table.reshape(-1), bias, q, kn, vn, ck, cv)


def _tail_body(x_ref, oa_ref, ol_ref, ga_ref, gb_ref, wua_ref, wul_ref, wo_ref, w1_ref, w2_ref,
               gpost_ref, gfpre_ref, gfpost_ref, y_ref):
    merged = (jax.nn.sigmoid(ga_ref[...]) * _dot(oa_ref[...], wua_ref[...])
              + jax.nn.sigmoid(gb_ref[...]) * _dot(ol_ref[...], wul_ref[...]))
    x = x_ref[...] + _rmsnorm(_dot(merged.astype(BF16), wo_ref[...]), gpost_ref[...])
    hf = _rmsnorm(x, gfpre_ref[...]).astype(BF16)
    f = jnp.square(jnp.maximum(_dot(hf, w1_ref[...]), 0.0)).astype(BF16)
    y_ref[...] = x + _rmsnorm(_dot(f, w2_ref[...]), gfpost_ref[...])


def _tail(x, oa, ol, ga, gb, wua, wul, wo, w1, w2, gpost, gfpre, gfpost, *, tm):
    n, d = x.shape
    row = lambda a: pl.BlockSpec((tm, a.shape[1]), lambda i: (i, 0))
    consts = (wua, wul, wo, w1, w2, gpost, gfpre, gfpost)
    return pl.pallas_call(
        _tail_body, grid=(n // tm,),
        in_specs=[row(x), row(oa), row(ol), row(ga), row(gb)] + [_const_spec(c.shape) for c in consts],
        out_specs=row(x), out_shape=jax.ShapeDtypeStruct((n, d), F32),
        compiler_params=_params(1), name="merge_ffn_tail",
    )(x, oa, ol, ga, gb, *consts)


def _row_tile(n, want):
    t = min(want, n)
    while n % t:
        t //= 2
    return t


def _head_pairs_t(a, n_blk, tq):
    heads, _, d = a.shape
    a = a.reshape(heads // 2, 2, n_blk, tq, d).transpose(2, 0, 4, 1, 3)
    return a.reshape(n_blk, heads // 2, d, 2 * tq)


def _layer(xp, xs, ck, cv, cik, page_base, h0, conv0, page_table, w):
    bp, sp, d_model = xp.shape
    nb, t_new, _ = xs.shape
    assert bp == 1, "prompt group is a single sequence"
    assert t_new == SUBLANES, "one sample sequence must fill one 8-row group"
    d_attn = N_HEADS * HEAD_DIM
    d_kv = N_KV_HEADS * HEAD_DIM
    d_lru = w["w_conv"].shape[1]
    dims = dict(d_attn=d_attn, d_kv=d_kv, d_lru=d_lru)

    w_in = w["w_in"]
    o_iq_end = d_attn + 2 * d_kv + IDX_HEADS * IDX_DIM
    o_iw_end = o_iq_end + IDX_HEADS
    o_ik_end = o_iw_end + IDX_DIM
    wa = w_in[:, :o_iq_end].astype(BF16)
    wi = jnp.concatenate([w_in[:, o_iw_end:o_ik_end], w_in[:, o_iq_end:o_iw_end],
                          jnp.zeros((d_model, LANES - IDX_DIM - IDX_HEADS), F32)], axis=1).astype(BF16)
    wl = w_in[:, o_ik_end:].astype(BF16)
    lru_consts = (w["w_conv"], w["b_conv"][None], w["w_rg"].astype(BF16), w["b_rg"][None],
                  w["w_ig"].astype(BF16), w["b_ig"][None], w["lru_lambda"][None])
    tail_consts = (w["w_up_attn"].astype(BF16), w["w_up_lru"].astype(BF16), w["w_out"].astype(BF16),
                   w["w_ff1"].astype(BF16), w["w_ff2"].astype(BF16),
                   w["g_mix_post"][None], w["g_ffn_pre"][None], w["g_ffn_post"][None])
    g_pre = w["g_mix_pre"][None]

    ns = nb * t_new
    xs2 = xs.reshape(ns, d_model)
    tms = _row_tile(ns, 256)
    q, k, v, kb, vb, iq, ikw, xl, xg, ga, gb = _in_projection(xs2, g_pre, wa, wi, wl, tm=tms, **dims)
    prev = jnp.pad(conv0, ((0, 0), (SUBLANES - (CONV_W - 1), 0), (0, 0))).reshape(ns, d_lru)
    h0x = jnp.broadcast_to(h0[:, None, :], (nb, t_new, d_lru)).reshape(ns, d_lru)
    o_lru, h_all = _lru_sample(xl, xg, prev, h0x, *lru_consts, tm=tms, seg=t_new)
    iq_s = iq.reshape(IDX_HEADS, nb, t_new, IDX_DIM).transpose(1, 0, 2, 3).reshape(nb, IDX_HEADS * t_new, IDX_DIM)
    iw_s = ikw[:, IDX_DIM:IDX_DIM + IDX_HEADS].reshape(nb, t_new, IDX_HEADS).transpose(0, 2, 1)
    iw_s = iw_s.reshape(nb, IDX_HEADS * t_new, 1)
    ikn = ikw[:, :IDX_DIM].reshape(nb, t_new, IDX_DIM)
    q_s = q.reshape(nb, t_new, N_KV_HEADS, GROUP, HEAD_DIM).transpose(0, 2, 3, 1, 4)
    q_s = q_s.reshape(nb, N_KV_HEADS, GROUP * t_new, HEAD_DIM)
    n_pages = page_table.shape[1]
    cp = math.gcd(n_pages, 16)
    bias = _sample_select(page_table, iq_s, iw_s, ikn, cik, cp=cp, page_base=page_base,
                          n_seq=math.gcd(nb, 16))
    o_att = _sample_attention(page_table, bias, q_s, k.reshape(nb, t_new, d_kv), v.reshape(nb, t_new, d_kv),
                              ck, cv, page=cik.shape[2], cp=cp, page_base=page_base)
    o_att = o_att.reshape(nb, N_KV_HEADS, GROUP, t_new, HEAD_DIM).transpose(0, 3, 1, 2, 4).reshape(ns, d_attn)
    ys = _tail(xs2, o_att, o_lru, ga, gb, *tail_consts, tm=tms).reshape(nb, t_new, d_model)
    st_s = (k.reshape(nb, t_new, N_KV_HEADS, HEAD_DIM), v.reshape(nb, t_new, N_KV_HEADS, HEAD_DIM), ikn,
            h_all.reshape(nb, t_new, d_lru)[:, -1], xl.reshape(nb, t_new, d_lru)[:, t_new - (CONV_W - 1):])

    xp2 = xp.reshape(sp, d_model)
    tmp = _row_tile(sp, 256)
    q, k, v, kb, vb, iq, ikw, xl, xg, ga, gb = _in_projection(xp2, g_pre, wa, wi, wl, tm=tmp, **dims)
    o_lru, h_last, tail3 = _lru_prompt(xl, xg, *lru_consts, tm=tmp)
    ik = ikw[:, :IDX_DIM]
    tq = _row_tile(sp, LANES)
    q2 = _head_pairs_t(q.reshape(sp, N_HEADS, HEAD_DIM).transpose(1, 0, 2), sp // tq, tq)
    o_att = _prompt_attention(_head_pairs_t(iq, sp // tq, tq), ikw[:, IDX_DIM:IDX_DIM + IDX_HEADS].T, q2,
                              ik.astype(BF16), kb, vb.T, tq=tq, kc=_row_tile(sp // 2, 1024))
    yp = _tail(xp2, o_att, o_lru, ga, gb, *tail_consts, tm=tmp).reshape(1, sp, d_model)
    st_p = (k.reshape(1, sp, N_KV_HEADS, HEAD_DIM), v.reshape(1, sp, N_KV_HEADS, HEAD_DIM),
            ik.reshape(1, sp, IDX_DIM), h_last[SUBLANES - 1:], tail3[None, SUBLANES - (CONV_W - 1):])
    return yp, ys, st_p, st_s


def kernel(x_prompt, x_sample, cache_k, cache_v, cache_idx_k, state_h, state_conv, page_table, w_in, w_conv, b_conv, w_rg, b_rg, w_ig, b_ig, lru_lambda, w_up_attn, w_up_lru, w_out, g_mix_pre, g_mix_post, g_ffn_pre, g_ffn_post, w_ff1, w_ff2):
    names = ("w_in", "w_conv", "b_conv", "w_rg", "b_rg", "w_ig", "b_ig", "lru_lambda", "w_up_attn",
             "w_up_lru", "w_out", "g_mix_pre", "g_mix_post", "g_ffn_pre", "g_ffn_post", "w_ff1", "w_ff2")
    stacked = (w_in, w_conv, b_conv, w_rg, b_rg, w_ig, b_ig, lru_lambda, w_up_attn, w_up_lru, w_out,
               g_mix_pre, g_mix_post, g_ffn_pre, g_ffn_post, w_ff1, w_ff2)
    depth, n_pool, page = cache_idx_k.shape[:3]
    ck = cache_k.reshape(depth * n_pool * page * N_KV_HEADS, HEAD_DIM)
    cv = cache_v.reshape(depth * n_pool * page * N_KV_HEADS, HEAD_DIM)
    cik = jnp.swapaxes(cache_idx_k, 2, 3).reshape(depth * n_pool, IDX_DIM, page)
    yp, ys = x_prompt, x_sample
    new_p, new_s = [], []
    for layer in range(depth):
        w = {n: a[layer] for n, a in zip(names, stacked)}
        yp, ys, st_p, st_s = _layer(yp, ys, ck, cv, cik, layer * n_pool, state_h[layer], state_conv[layer],
                                    page_table, w)
        new_p.append(st_p)
        new_s.append(st_s)
    stack = lambda states, j: jnp.stack([s[j] for s in states], axis=0)
    return (yp, ys) + tuple(stack(new_p, j) for j in range(5)) + tuple(stack(new_s, j) for j in range(5))
```

```python
import functools
import math

import jax
import jax.numpy as jnp
from jax import lax
from jax.experimental import pallas as pl
from jax.experimental.pallas import tpu as pltpu

N_HEADS = 8
HEAD_DIM = 128
N_KV_HEADS = 2
GROUP = N_HEADS // N_KV_HEADS
IDX_HEADS = 8
IDX_DIM = 64
TOPK_MAX = 256
LRU_BLOCKS = 8
CONV_W = 4
LRU_C = 8.0
EPS = 1e-6

LANES = 128
SUBLANES = 8
VMEM_LIMIT = 56 * 1024 * 1024

F32 = jnp.float32
BF16 = jnp.bfloat16
NEG_BIG = -1e30
IDX_BIG = 1e9
LOG2E = math.log2(math.e)


def _const_spec(shape):
    nd = len(shape)
    return pl.BlockSpec(shape, lambda *_: (0,) * nd, pipeline_mode=pl.Buffered(1))


def _params(n_grid):
    return pltpu.CompilerParams(dimension_semantics=("arbitrary",) * n_grid,
                                vmem_limit_bytes=VMEM_LIMIT)


def _rmsnorm(x, g):
    return x * lax.rsqrt(jnp.mean(x * x, axis=-1, keepdims=True) + EPS) * g


def _dot(a, b):
    return jnp.dot(a, b, preferred_element_type=F32)


def _dot_nt(a, b):
    return lax.dot_general(a, b, (((1,), (1,)), ((), ())), preferred_element_type=F32)


def _inproj_body(x_ref, g_ref, wa_ref, wi_ref, wl_ref,
                 q_ref, k_ref, v_ref, kb_ref, vb_ref, iq_ref, ikw_ref, xl_ref, xg_ref, ga_ref, gb_ref,
                 *, d_attn, d_kv, d_lru, d_model):
    hn = _rmsnorm(x_ref[...], g_ref[...]).astype(BF16)
    q_ref[...] = _dot(hn, wa_ref[:, :d_attn]).astype(BF16)
    k = _dot(hn, wa_ref[:, d_attn:d_attn + d_kv])
    k_ref[...] = k
    kb_ref[...] = k.astype(BF16)
    v = _dot(hn, wa_ref[:, d_attn + d_kv:d_attn + 2 * d_kv])
    v_ref[...] = v
    vb_ref[...] = v.astype(BF16)
    iq = _dot(hn, wa_ref[:, d_attn + 2 * d_kv:])
    for h in range(IDX_HEADS):
        iq_ref[h] = iq[:, h * IDX_DIM:(h + 1) * IDX_DIM].astype(BF16)
    ikw_ref[...] = _dot(hn, wi_ref[...])
    xl_ref[...] = _dot(hn, wl_ref[:, :d_lru])
    xg_ref[...] = _dot(hn, wl_ref[:, d_lru:2 * d_lru])
    ga_ref[...] = _dot(hn, wl_ref[:, 2 * d_lru:2 * d_lru + d_model])
    gb_ref[...] = _dot(hn, wl_ref[:, 2 * d_lru + d_model:])


def _in_projection(x, g, wa, wi, wl, *, d_attn, d_kv, d_lru, tm):
    n, d_model = x.shape
    grid = (n // tm,)
    row = lambda w: pl.BlockSpec((tm, w), lambda i: (i, 0))
    out_shape = (
        jax.ShapeDtypeStruct((n, d_attn), BF16),
        jax.ShapeDtypeStruct((n, d_kv), F32),
        jax.ShapeDtypeStruct((n, d_kv), F32),
        jax.ShapeDtypeStruct((n, d_kv), BF16),
        jax.ShapeDtypeStruct((n, d_kv), BF16),
        jax.ShapeDtypeStruct((IDX_HEADS, n, IDX_DIM), BF16),
        jax.ShapeDtypeStruct((n, LANES), F32),
        jax.ShapeDtypeStruct((n, d_lru), F32),
        jax.ShapeDtypeStruct((n, d_lru), F32),
        jax.ShapeDtypeStruct((n, d_model), F32),
        jax.ShapeDtypeStruct((n, d_model), F32),
    )
    out_specs = (row(d_attn), row(d_kv), row(d_kv), row(d_kv), row(d_kv),
                 pl.BlockSpec((IDX_HEADS, tm, IDX_DIM), lambda i: (0, i, 0)),
                 row(LANES), row(d_lru), row(d_lru), row(d_model), row(d_model))
    body = functools.partial(_inproj_body, d_attn=d_attn, d_kv=d_kv, d_lru=d_lru, d_model=d_model)
    return pl.pallas_call(
        body, grid=grid,
        in_specs=[row(d_model), _const_spec(g.shape), _const_spec(wa.shape), _const_spec(wi.shape),
                  _const_spec(wl.shape)],
        out_specs=out_specs, out_shape=out_shape, compiler_params=_params(1), name="in_projection",
    )(x, g, wa, wi, wl)


def _shift_rows(cur, prev, j, row_in_seg):
    tm = cur.shape[0]
    from_prev = pltpu.roll(prev, tm - SUBLANES + j, axis=0)
    return jnp.where(row_in_seg < j, from_prev, pltpu.roll(cur, j, axis=0))


def _lru_core(xl, xg, prev, h_in, wc, bc, wrg, brg, wig, big, lam, *, seg):
    tm, d = xl.shape
    row = lax.broadcasted_iota(jnp.int32, (tm, 1), 0)
    row_in_seg = row % seg
    u = bc + wc[CONV_W - 1:CONV_W] * xl
    for j in range(1, CONV_W):
        u = u + wc[CONV_W - 1 - j:CONV_W - j] * _shift_rows(xl, prev, j, row_in_seg)
    ub = u.astype(BF16)
    blk = d // LRU_BLOCKS
    r = jnp.concatenate([_dot(ub[:, n * blk:(n + 1) * blk], wrg[n]) for n in range(LRU_BLOCKS)], axis=1)
    ig = jnp.concatenate([_dot(ub[:, n * blk:(n + 1) * blk], wig[n]) for n in range(LRU_BLOCKS)], axis=1)
    r = jax.nn.sigmoid(r + brg)
    ig = jax.nn.sigmoid(ig + big)
    softplus_neg = jnp.maximum(-lam, 0.0) + jnp.log1p(jnp.exp(-jnp.abs(lam)))
    log_a = -LRU_C * r * softplus_neg
    a = jnp.exp(log_a)
    b = jnp.sqrt(-jnp.tanh(log_a) * (a * a + 1.0)) * (ig * u)
    d_step = 1
    while d_step < seg:
        keep = row_in_seg >= d_step
        a_sh = jnp.where(keep, pltpu.roll(a, d_step, axis=0), 1.0)
        b_sh = jnp.where(keep, pltpu.roll(b, d_step, axis=0), 0.0)
        b = a * b_sh + b
        a = a * a_sh
        d_step *= 2
    h = a * h_in + b
    gelu = 0.5 * xg * (1.0 + jnp.tanh(math.sqrt(2.0 / math.pi) * (xg + 0.044715 * (xg * xg * xg))))
    return h * gelu, h


def _lru_prompt_body(xl_ref, xg_ref, wc_ref, bc_ref, wrg_ref, brg_ref, wig_ref, big_ref, lam_ref,
                     o_ref, h_ref, tail_ref, hc_sc, prev_sc):
    @pl.when(pl.program_id(0) == 0)
    def _():
        hc_sc[...] = jnp.zeros_like(hc_sc)
        prev_sc[...] = jnp.zeros_like(prev_sc)

    xl = xl_ref[...]
    tm = xl.shape[0]
    prev = jnp.tile(prev_sc[...], (tm // SUBLANES, 1))
    out, h = _lru_core(xl, xg_ref[...], prev, hc_sc[0:1, :], wc_ref[...], bc_ref[...], wrg_ref[...],
                       brg_ref[...], wig_ref[...], big_ref[...], lam_ref[...], seg=tm)
    o_ref[...] = out.astype(BF16)
    last = h[tm - SUBLANES:, :]
    hc_sc[...] = jnp.broadcast_to(last[SUBLANES - 1:SUBLANES, :], hc_sc.shape)
    h_ref[...] = last
    prev_sc[...] = xl[tm - SUBLANES:, :]
    tail_ref[...] = xl[tm - SUBLANES:, :]


def _lru_prompt(xl, xg, wc, bc, wrg, brg, wig, big, lam, *, tm):
    n, d = xl.shape
    row = pl.BlockSpec((tm, d), lambda i: (i, 0))
    last8 = pl.BlockSpec((SUBLANES, d), lambda i: (0, 0))
    consts = (wc, bc, wrg, brg, wig, big, lam)
    return pl.pallas_call(
        _lru_prompt_body, grid=(n // tm,),
        in_specs=[row, row] + [_const_spec(c.shape) for c in consts],
        out_specs=(row, last8, last8),
        out_shape=(jax.ShapeDtypeStruct((n, d), BF16),
                   jax.ShapeDtypeStruct((SUBLANES, d), F32),
                   jax.ShapeDtypeStruct((SUBLANES, d), F32)),
        scratch_shapes=[pltpu.VMEM((SUBLANES, d), F32), pltpu.VMEM((SUBLANES, d), F32)],
        compiler_params=_params(1), name="lru_prompt",
    )(xl, xg, *consts)


def _lru_sample_body(xl_ref, xg_ref, prev_ref, h0_ref, wc_ref, bc_ref, wrg_ref, brg_ref, wig_ref,
                     big_ref, lam_ref, o_ref, h_ref, *, seg):
    out, h = _lru_core(xl_ref[...], xg_ref[...], prev_ref[...], h0_ref[...], wc_ref[...], bc_ref[...],
                       wrg_ref[...], brg_ref[...], wig_ref[...], big_ref[...], lam_ref[...], seg=seg)
    o_ref[...] = out.astype(BF16)
    h_ref[...] = h


def _lru_sample(xl, xg, prev, h0, wc, bc, wrg, brg, wig, big, lam, *, tm, seg):
    n, d = xl.shape
    row = pl.BlockSpec((tm, d), lambda i: (i, 0))
    consts = (wc, bc, wrg, brg, wig, big, lam)
    return pl.pallas_call(
        functools.partial(_lru_sample_body, seg=seg), grid=(n // tm,),
        in_specs=[row, row, row, row] + [_const_spec(c.shape) for c in consts],
        out_specs=(row, row),
        out_shape=(jax.ShapeDtypeStruct((n, d), BF16), jax.ShapeDtypeStruct((n, d), F32)),
        compiler_params=_params(1), name="lru_sample",
    )(xl, xg, prev, h0, *consts)


def _ordered_bits_to_float(u):
    key = u ^ jnp.int32(-2 ** 31)
    bits = jnp.where(key >= 0, key, key ^ jnp.int32(0x7FFFFFFF))
    return lax.bitcast_convert_type(bits, F32)


REFINE_STEPS = 12


def _float_to_ordered_bits(x):
    bits = lax.bitcast_convert_type(x, jnp.int32)
    key = jnp.where(bits >= 0, bits, bits ^ jnp.int32(0x7FFFFFFF))
    return key ^ jnp.int32(-2 ** 31)


def _kth_search(count_ge, k_eff, bracket=None):
    shape = k_eff.shape
    zero = jnp.zeros(shape, jnp.int32)
    if bracket is None:
        i0, u0 = jnp.int32(0), zero
    else:
        u_lo, u_hi = (_float_to_ordered_bits(x) for x in bracket)
        i0 = jnp.minimum(jnp.min(lax.clz(u_lo ^ u_hi)), 31)
        keep = lax.shift_left(lax.shift_left(jnp.int32(-1), 31 - i0), 1)
        u0 = u_hi & keep

    def cond(st):
        return (st[0] < 32) & (st[4] == 0)

    def body(st):
        i, u, thr, exact, _ = st
        cand = u | lax.shift_left(jnp.int32(1), 31 - i)
        t = _ordered_bits_to_float(cand)
        c = count_ge(t)
        hit = (c == k_eff) & (exact == 0)
        thr = jnp.where(hit, t, thr)
        exact = jnp.where(hit, 1, exact)
        u = jnp.where(c >= k_eff, cand, u)
        return i + 1, u, thr, exact, jnp.min(exact)

    init = (i0, u0, jnp.zeros(shape, F32), zero, jnp.int32(0))
    _, u, thr, exact, all_exact = lax.while_loop(cond, body, init)

    def refine_cond(st):
        return (st[0] < REFINE_STEPS) & (st[5] == 0)

    def refine(st):
        j, lo, hi, thr, exact, _ = st
        mid = lo + (hi - lo) * 0.5
        c = count_ge(mid)
        hit = (c == k_eff) & (exact == 0)
        thr = jnp.where(hit, mid, thr)
        exact = jnp.where(hit, 1, exact)
        return j + 1, jnp.where(c > k_eff, mid, lo), jnp.where(c < k_eff, mid, hi), thr, exact, jnp.min(exact)

    init = (jnp.int32(0), _ordered_bits_to_float(u), _ordered_bits_to_float(u + 1), thr, exact, all_exact)
    _, lo, _, thr, exact, _ = lax.while_loop(refine_cond, refine, init)
    return jnp.where(exact == 1, thr, lo), exact


BRACKET_BITS = 24


def _bracket_search(count_ge, k_eff, bracket):
    lo, hi = bracket
    lo = jnp.maximum(lo, NEG_BIG)
    width = hi - lo
    shape = k_eff.shape
    zero = jnp.zeros(shape, jnp.int32)

    def cond(st):
        return (st[0] < BRACKET_BITS) & (st[4] == 0)

    def body(st):
        i, f, thr, exact, _ = st
        cand = f | lax.shift_left(jnp.int32(1), BRACKET_BITS - 1 - i)
        t = lo + (cand.astype(F32) * 2.0 ** -BRACKET_BITS) * width
        c = count_ge(t)
        hit = (c == k_eff) & (exact == 0)
        thr = jnp.where(hit, t, thr)
        exact = jnp.where(hit, 1, exact)
        f = jnp.where(c >= k_eff, cand, f)
        return i + 1, f, thr, exact, jnp.min(exact)

    init = (jnp.int32(0), zero, jnp.zeros(shape, F32), zero, jnp.int32(0))
    _, f, thr, exact, _ = lax.while_loop(cond, body, init)
    top = f == 2 ** BRACKET_BITS - 1
    t_above = jnp.where(top, jnp.inf, lo + ((f + 1).astype(F32) * 2.0 ** -BRACKET_BITS) * width)
    return thr, exact, t_above


def _tie_cut(count_lt, k_eff, n_bits):
    def step(i, j):
        cand = j | lax.shift_left(jnp.int32(1), n_bits - 1 - i)
        ok = count_lt(cand.astype(F32)) < k_eff
        return jnp.where(ok, cand, j)
    j = lax.fori_loop(0, n_bits, step, jnp.zeros(k_eff.shape, jnp.int32))
    return j.astype(F32)


def _rank(score, thr, key_idx):
    return jnp.where(score > thr, -1.0, jnp.where(score == thr, key_idx, IDX_BIG))


def _tree_sum(terms):
    while len(terms) > 1:
        terms = [a + b for a, b in zip(terms[0::2], terms[1::2])] + ([terms[-1]] if len(terms) % 2 else [])
    return terms[0]


COUNT_ROWS = 64


def _pattn_body(iq2_ref, iwt_ref, q2_ref, ik_ref, k_ref, vt_ref, o_ref,
                s_sc, cm_sc, sa_sc, sb_sc, pa_sc, pb_sc, thr_sc, m_sc, l_sc, acc_sc,
                *, tq, kc, n_sel, n_bits):
    qb = pl.program_id(0)
    n_chunks = ((qb + 1) * tq + kc - 1) // kc
    n_chunks_even = (n_chunks + 1) // 2 * 2
    q_pos = qb * tq + lax.broadcasted_iota(jnp.int32, (1, tq), 1)
    n_pairs = N_HEADS // 2
    iw = iwt_ref[...] * (IDX_DIM ** -0.5 * IDX_HEADS ** -0.5)

    def chunk_off(c):
        return pl.multiple_of(c * kc, kc)

    def score_chunk(c, carry):
        off = chunk_off(c)
        ikc = ik_ref[pl.ds(off, kc), :]
        acc = jnp.zeros((kc, tq), F32)
        for hp in range(IDX_HEADS // 2):
            lg = _dot(ikc, iq2_ref[0, hp])
            acc = acc + jnp.maximum(lg[:, :tq], 0.0) * iw[2 * hp:2 * hp + 1, :]
            acc = acc + jnp.maximum(lg[:, tq:], 0.0) * iw[2 * hp + 1:2 * hp + 2, :]
        k_pos = off + lax.broadcasted_iota(jnp.int32, (kc, 1), 0)
        sc = jnp.where(k_pos <= q_pos, acc, -jnp.inf)
        s_sc[pl.ds(off, kc), :] = sc
        cm_sc[...] = functools.reduce(jnp.maximum, [cm_sc[...]] + [
            sc[j * TOPK_MAX:(j + 1) * TOPK_MAX] for j in range(kc // TOPK_MAX)])
        return carry
    cm_sc[...] = jnp.full_like(cm_sc, -jnp.inf)
    lax.fori_loop(0, n_chunks_even, score_chunk, 0)
    bracket = (jnp.min(cm_sc[...], axis=0, keepdims=True), jnp.max(cm_sc[...], axis=0, keepdims=True))

    def count_where(pred):
        def body(c, acc):
            off = chunk_off(c)
            for j in range(kc // COUNT_ROWS):
                acc = acc + jnp.where(pred(s_sc[pl.ds(off + j * COUNT_ROWS, COUNT_ROWS), :]), 1, 0)
            return acc
        acc = lax.fori_loop(0, n_chunks, body, jnp.zeros((COUNT_ROWS, tq), jnp.int32))
        return jnp.sum(acc, axis=0, keepdims=True)

    k_eff = jnp.minimum(q_pos + 1, n_sel)
    def max_below(t):
        def body(c, acc):
            off = chunk_off(c)
            for j in range(kc // COUNT_ROWS):
                blk = s_sc[pl.ds(off + j * COUNT_ROWS, COUNT_ROWS), :]
                acc = jnp.maximum(acc, jnp.where(blk < t, blk, -jnp.inf))
            return acc
        acc = lax.fori_loop(0, n_chunks, body, jnp.full((COUNT_ROWS, tq), -jnp.inf, F32))
        return jnp.max(acc, axis=0, keepdims=True)

    def break_ties(thr):
        def rank_chunk(c, carry):
            off = chunk_off(c)
            key_idx = (off + lax.broadcasted_iota(jnp.int32, (kc, 1), 0)).astype(F32)
            s_sc[pl.ds(off, kc), :] = _rank(s_sc[pl.ds(off, kc), :], thr, key_idx)
            return carry
        lax.fori_loop(0, n_chunks, rank_chunk, 0)
        cut = _tie_cut(lambda j: count_where(lambda blk: blk < j), k_eff, n_bits)

        def recode_chunk(c, carry):
            off = chunk_off(c)
            s_sc[pl.ds(off, kc), :] = jnp.where(s_sc[pl.ds(off, kc), :] <= cut, 1.0, -1.0)
            return carry
        lax.fori_loop(0, n_chunks, recode_chunk, 0)
        thr_sc[...] = jnp.zeros_like(thr_sc)

    count_ge = lambda t: count_where(lambda blk: blk >= t)
    thr_b, exact_b, t_above = _bracket_search(count_ge, k_eff, bracket)
    thr_sc[...] = jnp.broadcast_to(thr_b, thr_sc.shape)

    @pl.when(jnp.min(exact_b) == 0)
    def _():
        v = max_below(t_above)
        n_ge = count_ge(v)
        n_gt = count_where(lambda blk: blk > v)
        found = (exact_b == 1) | ((n_gt < k_eff) & (n_ge >= k_eff))
        thr_v = jnp.where(exact_b == 1, thr_b, v)
        tied = (exact_b == 0) & (n_ge > k_eff)
        thr_sc[...] = jnp.broadcast_to(thr_v, thr_sc.shape)
        all_found = jnp.min(found.astype(jnp.int32))

        @pl.when((all_found == 1) & (jnp.max(tied.astype(jnp.int32)) == 1))
        def _():
            break_ties(thr_v)

        @pl.when(all_found == 0)
        def _():
            thr, exact = _kth_search(count_ge, k_eff, bracket)
            thr_sc[...] = jnp.broadcast_to(thr, thr_sc.shape)

            @pl.when(jnp.min(exact) == 0)
            def _():
                break_ties(thr)

    sel_thr = thr_sc[0:1, :]
    m_sc[...] = jnp.full_like(m_sc, NEG_BIG)
    l_sc[...] = jnp.zeros_like(l_sc)
    acc_sc[...] = jnp.zeros_like(acc_sc)
    c2 = HEAD_DIM ** -0.5 * LOG2E
    last_chunk = k_ref.shape[0] // kc - 1

    def qk(c, s_buf):
        off = chunk_off(jnp.minimum(c, last_chunk))
        for hp in range(n_pairs):
            g = (2 * hp) // GROUP
            s_buf[:, 2 * hp * tq:(2 * hp + 2) * tq] = _dot(
                k_ref[pl.ds(off, kc), g * HEAD_DIM:(g + 1) * HEAD_DIM], q2_ref[0, hp])

    def pv(c, p_buf, alphas):
        off = chunk_off(jnp.maximum(c, 0))
        for hp in range(n_pairs):
            g = (2 * hp) // GROUP
            acc_sc[hp] = alphas[hp] * acc_sc[hp] + _dot(
                vt_ref[g * HEAD_DIM:(g + 1) * HEAD_DIM, pl.ds(off, kc)],
                p_buf[:, 2 * hp * tq:(2 * hp + 2) * tq])

    def softmax(c, s_buf, p_buf):
        bias = jnp.where(s_sc[pl.ds(chunk_off(c), kc), :] >= sel_thr, 0.0, NEG_BIG)
        alphas = []
        for h in range(N_HEADS):
            s = s_buf[:, h * tq:(h + 1) * tq] + bias
            s_buf[:, h * tq:(h + 1) * tq] = s
            m_old = m_sc[h]
            m_new = jnp.maximum(m_old, jnp.max(s, axis=0, keepdims=True))
            alpha = jnp.exp2((m_old - m_new) * c2)
            p = jnp.exp2((s_buf[:, h * tq:(h + 1) * tq] - m_new[0:1, :]) * c2)
            l_sc[h] = alpha * l_sc[h] + jnp.sum(p, axis=0, keepdims=True)
            m_sc[h] = m_new
            p_buf[:, h * tq:(h + 1) * tq] = p.astype(BF16)
            alphas.append(alpha[0:1, :])
        return [jnp.concatenate(alphas[2 * hp:2 * hp + 2], axis=1) for hp in range(n_pairs)]

    def stage(c, s_cur, p_cur, s_next, p_prev, alphas_prev):
        qk(c + 1, s_next)
        pv(c - 1, p_prev, alphas_prev)
        return softmax(c, s_cur, p_cur)

    qk(0, sa_sc)
    pb_sc[...] = jnp.zeros_like(pb_sc)

    def attn_two_chunks(i, alphas):
        alphas = stage(2 * i, sa_sc, pa_sc, sb_sc, pb_sc, alphas)
        return stage(2 * i + 1, sb_sc, pb_sc, sa_sc, pa_sc, alphas)
    ones = [jnp.ones((1, 2 * tq), F32)] * n_pairs
    alphas = lax.fori_loop(0, n_chunks_even // 2, attn_two_chunks, ones)
    pv(n_chunks_even - 1, pb_sc, alphas)

    for h in range(N_HEADS):
        o_t = acc_sc[h // 2][:, (h % 2) * tq:(h % 2 + 1) * tq] / l_sc[h][0:1, :]
        o_ref[:, h * HEAD_DIM:(h + 1) * HEAD_DIM] = o_t.T.astype(BF16)


def _prompt_attention(iq2, iwt, q2, ik, kb, vt, *, tq, kc):
    n = kb.shape[0]
    d_attn = N_HEADS * HEAD_DIM
    n_sel = min(TOPK_MAX, n // 4)
    n_bits = max(1, (n - 1).bit_length())
    assert (n // kc) % 2 == 0, "the attention pipeline consumes key chunks in pairs"
    assert kc % TOPK_MAX == 0, "a key chunk holds whole sets of residue classes"
    body = functools.partial(_pattn_body, tq=tq, kc=kc, n_sel=n_sel, n_bits=n_bits)
    blk4 = lambda a: pl.BlockSpec((1,) + a.shape[1:], lambda i: (i, 0, 0, 0))
    return pl.pallas_call(
        body, grid=(n // tq,),
        in_specs=[blk4(iq2), pl.BlockSpec((IDX_HEADS, tq), lambda i: (0, i)), blk4(q2),
                  _const_spec(ik.shape), _const_spec(kb.shape), _const_spec(vt.shape)],
        out_specs=pl.BlockSpec((tq, d_attn), lambda i: (i, 0)),
        out_shape=jax.ShapeDtypeStruct((n, d_attn), BF16),
        scratch_shapes=[pltpu.VMEM((n, tq), F32),
                        pltpu.VMEM((TOPK_MAX, tq), F32),
                        pltpu.VMEM((kc, N_HEADS * tq), F32),
                        pltpu.VMEM((kc, N_HEADS * tq), F32),
                        pltpu.VMEM((kc, N_HEADS * tq), BF16),
                        pltpu.VMEM((kc, N_HEADS * tq), BF16),
                        pltpu.VMEM((SUBLANES, tq), F32),
                        pltpu.VMEM((N_HEADS, SUBLANES, tq), F32),
                        pltpu.VMEM((N_HEADS, SUBLANES, tq), F32),
                        pltpu.VMEM((N_HEADS // 2, HEAD_DIM, 2 * tq), F32)],
        compiler_params=_params(1), name="prompt_attention",
    )(iq2, iwt, q2, ik, kb, vt)


def _ssel_body(pt_ref, iq_ref, iw_ref, ikn_ref, cik_hbm, o_ref,
               ik_buf, s_sc, snew_sc, cm_sc, thr_sc, ik_sem,
               *, n_pages, page, cp, t_new, n_sel, n_bits, page_base, n_seq):
    grp = pl.program_id(0)
    nb = pl.num_programs(0) * n_seq
    past = n_pages * page
    n_ch = n_pages // cp
    kc = cp * page
    rows = n_seq * t_new

    def start_ik(seq, slot):
        def body(p, c):
            pltpu.make_async_copy(cik_hbm.at[pt_ref[seq * n_pages + p] + page_base],
                                  ik_buf.at[slot, :, pl.ds(p * page, page)], ik_sem.at[slot]).start()
            return c
        lax.fori_loop(0, n_pages, body, 0)

    def wait_ik(slot):
        pltpu.make_async_copy(ik_buf.at[slot], ik_buf.at[slot], ik_sem.at[slot]).wait()

    @pl.when(grp == 0)
    def _():
        start_ik(0, 0)

    t_idx = lax.broadcasted_iota(jnp.int32, (t_new, LANES), 0)
    j_idx = lax.broadcasted_iota(jnp.int32, (t_new, LANES), 1)

    def head_sum(x):
        return _tree_sum([x[h * t_new:(h + 1) * t_new] for h in range(IDX_HEADS)])

    def score_seq(i, carry):
        seq = grp * n_seq + i
        slot = seq % 2

        @pl.when(seq + 1 < nb)
        def _():
            start_ik(seq + 1, 1 - slot)

        wait_ik(slot)
        iq = iq_ref[i]
        iw = iw_ref[i] * (IDX_DIM ** -0.5 * IDX_HEADS ** -0.5)
        r0 = pl.multiple_of(i * t_new, t_new)
        cm = [jnp.full((t_new, LANES), -jnp.inf, F32)] * 2
        for c in range(n_ch):
            ikc = ik_buf[slot, :, c * kc:(c + 1) * kc].astype(BF16)
            sc = head_sum(jnp.maximum(_dot(iq, ikc), 0.0) * iw)
            s_sc[pl.ds(r0, t_new), c * kc:(c + 1) * kc] = sc
            for j in range(kc // LANES):
                cm[j % 2] = jnp.maximum(cm[j % 2], sc[:, j * LANES:(j + 1) * LANES])
        cm_sc[pl.ds(r0, t_new), :] = jnp.concatenate(cm, axis=1)
        ikn = jnp.concatenate([ikn_ref[i], jnp.zeros((LANES - t_new, IDX_DIM), F32)], axis=0)
        s_new = head_sum(jnp.maximum(_dot_nt(iq, ikn.astype(BF16)), 0.0) * iw)
        snew_sc[pl.ds(r0, t_new), :] = jnp.where(j_idx <= t_idx, s_new, -jnp.inf)
        return carry
    lax.fori_loop(0, n_seq, score_seq, 0)

    def count_where(pred):
        terms = [jnp.where(pred(snew_sc[...]), 1, 0)]
        for j in range(past // LANES):
            terms.append(jnp.where(pred(s_sc[:, j * LANES:(j + 1) * LANES]), 1, 0))
        return jnp.sum(_tree_sum(terms), axis=1, keepdims=True)

    k_eff = jnp.full((rows, 1), n_sel, jnp.int32)
    assert n_sel <= 2 * LANES <= past
    lo = jnp.min(cm_sc[...], axis=1, keepdims=True)
    hi = jnp.maximum(jnp.max(cm_sc[...], axis=1, keepdims=True), jnp.max(snew_sc[...], axis=1, keepdims=True))
    count_ge = lambda t: count_where(lambda blk: blk >= t)
    thr_b, exact_b, _ = _bracket_search(count_ge, k_eff, (lo, hi))
    thr_sc[...] = jnp.broadcast_to(thr_b, thr_sc.shape)

    @pl.when(jnp.min(exact_b) == 0)
    def _():
        thr, exact = _kth_search(count_ge, k_eff, (lo, hi))
        thr_sc[...] = jnp.broadcast_to(thr, thr_sc.shape)

        @pl.when(jnp.min(exact) == 0)
        def _():
            lane_f = lax.broadcasted_iota(jnp.int32, (1, kc), 1)
            j_all = lax.broadcasted_iota(jnp.int32, (1, LANES), 1)
            for c in range(n_ch):
                s_sc[:, c * kc:(c + 1) * kc] = _rank(s_sc[:, c * kc:(c + 1) * kc], thr,
                                                     (c * kc + lane_f).astype(F32))
            snew_sc[...] = _rank(snew_sc[...], thr, (past + j_all).astype(F32))
            cut = _tie_cut(lambda j: count_where(lambda blk: blk < j), k_eff, n_bits)
            for c in range(n_ch):
                s_sc[:, c * kc:(c + 1) * kc] = jnp.where(s_sc[:, c * kc:(c + 1) * kc] <= cut, 1.0, -1.0)
            snew_sc[...] = jnp.where(snew_sc[...] <= cut, 1.0, -1.0)
            thr_sc[...] = jnp.zeros_like(thr_sc)

    sel_thr = thr_sc[:, 0:1]
    for c in range(n_ch):
        o_ref[:, c * kc:(c + 1) * kc] = jnp.where(s_sc[:, c * kc:(c + 1) * kc] >= sel_thr, 0.0, NEG_BIG)
    o_ref[:, past:] = jnp.where(snew_sc[...] >= sel_thr, 0.0, NEG_BIG)


def _sample_select(page_table, iq, iw, ikn, cik, *, cp, page_base, n_seq):
    nb, n_pages = page_table.shape
    page = cik.shape[2]
    t_new = ikn.shape[1]
    past = n_pages * page
    n_sel = min(TOPK_MAX, (past + t_new) // 4)
    n_bits = max(1, (past + t_new - 1).bit_length())
    rows = n_seq * t_new
    body = functools.partial(_ssel_body, n_pages=n_pages, page=page, cp=cp, t_new=t_new,
                             n_sel=n_sel, n_bits=n_bits, page_base=page_base, n_seq=n_seq)
    grp3 = lambda a: pl.BlockSpec((n_seq,) + a.shape[1:], lambda i, pt: (i, 0, 0))
    grid_spec = pltpu.PrefetchScalarGridSpec(
        num_scalar_prefetch=1, grid=(nb // n_seq,),
        in_specs=[grp3(iq), grp3(iw), grp3(ikn), pl.BlockSpec(memory_space=pl.ANY)],
        out_specs=pl.BlockSpec((rows, past + LANES), lambda i, pt: (i, 0)),
        scratch_shapes=[pltpu.VMEM((2, IDX_DIM, past), F32),
                        pltpu.VMEM((rows, past), F32),
                        pltpu.VMEM((rows, LANES), F32),
                        pltpu.VMEM((rows, 2 * LANES), F32),
                        pltpu.VMEM((rows, LANES), F32),
                        pltpu.SemaphoreType.DMA((2,))],
    )
    return pl.pallas_call(
        body, grid_spec=grid_spec,
        out_shape=jax.ShapeDtypeStruct((nb * t_new, past + LANES), F32),
        compiler_params=_params(1), name="sample_select",
    )(page_table.reshape(-1), iq, iw, ikn, cik)


def _sattn_body(pt_ref, bias_ref, q_ref, kn_ref, vn_ref, ck_hbm, cv_hbm,
                o_ref, k_buf, v_buf, k_sem, v_sem, *, n_pages, page, cp, t_new, page_base):
    b = pl.program_id(0)
    nb = pl.num_programs(0)
    past = n_pages * page
    n_ch = n_pages // cp
    kc = cp * page
    page_rows = page * N_KV_HEADS

    def start_kv(seq, chunk, slot):
        def body(p, c):
            src = pl.ds((pt_ref[seq * n_pages + chunk * cp + p] + page_base) * page_rows, page_rows)
            dst = pl.ds(p * page_rows, page_rows)
            pltpu.make_async_copy(ck_hbm.at[src, :], k_buf.at[slot, dst, :], k_sem.at[slot]).start()
            pltpu.make_async_copy(cv_hbm.at[src, :], v_buf.at[slot, dst, :], v_sem.at[slot]).start()
            return c
        lax.fori_loop(0, cp, body, 0)

    def wait_kv(slot):
        pltpu.make_async_copy(k_buf.at[slot], k_buf.at[slot], k_sem.at[slot]).wait()
        pltpu.make_async_copy(v_buf.at[slot], v_buf.at[slot], v_sem.at[slot]).wait()

    @pl.when(b == 0)
    def _():
        start_kv(0, 0, 0)

    def pad_keys(x):
        return jnp.concatenate([x, jnp.zeros((LANES - t_new, x.shape[1]), F32)], axis=0).astype(BF16)

    c2 = HEAD_DIM ** -0.5 * LOG2E
    bias_new = jnp.tile(bias_ref[:, past:], (GROUP, 1))
    m = []
    l = []
    acc = []
    for g in range(N_KV_HEADS):
        kn = pad_keys(kn_ref[0, :, g * HEAD_DIM:(g + 1) * HEAD_DIM])
        vn = pad_keys(vn_ref[0, :, g * HEAD_DIM:(g + 1) * HEAD_DIM])
        s = _dot_nt(q_ref[0, g], kn) + bias_new
        m_g = jnp.max(s, axis=1, keepdims=True)
        p = jnp.exp2((s - m_g) * c2)
        m.append(m_g)
        l.append(jnp.sum(p, axis=1, keepdims=True))
        acc.append(_dot(p.astype(BF16), vn))

    for c in range(n_ch):
        slot = (b * n_ch + c) % 2
        if c + 1 < n_ch:
            start_kv(b, c + 1, 1 - slot)
        else:
            @pl.when(b + 1 < nb)
            def _():
                start_kv(b + 1, 0, 1 - slot)
        wait_kv(slot)
        bias = jnp.tile(bias_ref[:, c * kc:(c + 1) * kc], (GROUP, 1))
        for g in range(N_KV_HEADS):
            kg = k_buf[slot, pl.ds(g, kc, stride=N_KV_HEADS), :].astype(BF16)
            vg = v_buf[slot, pl.ds(g, kc, stride=N_KV_HEADS), :].astype(BF16)
            s = _dot_nt(q_ref[0, g], kg) + bias
            m_new = jnp.maximum(m[g], jnp.max(s, axis=1, keepdims=True))
            alpha = jnp.exp2((m[g] - m_new) * c2)
            p = jnp.exp2((s - m_new) * c2)
            l[g] = alpha * l[g] + jnp.sum(p, axis=1, keepdims=True)
            acc[g] = alpha * acc[g] + _dot(p.astype(BF16), vg)
            m[g] = m_new

    for g in range(N_KV_HEADS):
        o_ref[0, g] = (acc[g] / l[g]).astype(BF16)


def _sample_attention(page_table, bias, q, kn, vn, ck, cv, *, page, cp, page_base):
    nb, n_pages = page_table.shape
    t_new = kn.shape[1]
    rows = GROUP * t_new
    body = functools.partial(_sattn_body, n_pages=n_pages, page=page, cp=cp, t_new=t_new,
                             page_base=page_base)
    seq3 = lambda a: pl.BlockSpec((1,) + a.shape[1:], lambda i, pt: (i, 0, 0))
    seq4 = lambda a: pl.BlockSpec((1,) + a.shape[1:], lambda i, pt: (i, 0, 0, 0))
    any_spec = pl.BlockSpec(memory_space=pl.ANY)
    grid_spec = pltpu.PrefetchScalarGridSpec(
        num_scalar_prefetch=1, grid=(nb,),
        in_specs=[pl.BlockSpec((t_new, bias.shape[1]), lambda i, pt: (i, 0)), seq4(q), seq3(kn), seq3(vn),
                  any_spec, any_spec],
        out_specs=pl.BlockSpec((1, N_KV_HEADS, rows, HEAD_DIM), lambda i, pt: (i, 0, 0, 0)),
        scratch_shapes=[pltpu.VMEM((2, cp * page * N_KV_HEADS, HEAD_DIM), F32),
                        pltpu.VMEM((2, cp * page * N_KV_HEADS, HEAD_DIM), F32),
                        pltpu.SemaphoreType.DMA((2,)),
                        pltpu.SemaphoreType.DMA((2,))],
    )
    return pl.pallas_call(
        body, grid_spec=grid_spec,
        out_shape=jax.ShapeDtypeStruct((nb, N_KV_HEADS, rows, HEAD_DIM), BF16),
        compiler_params=_params(1), name="sample_attention",
    )(page_table.reshape(-1), bias, q, kn, vn, ck, cv)


def _tail_body(x_ref, oa_ref, ol_ref, ga_ref, gb_ref, wua_ref, wul_ref, wo_ref, w1_ref, w2_ref,
               gpost_ref, gfpre_ref, gfpost_ref, y_ref):
    merged = (jax.nn.sigmoid(ga_ref[...]) * _dot(oa_ref[...], wua_ref[...])
              + jax.nn.sigmoid(gb_ref[...]) * _dot(ol_ref[...], wul_ref[...]))
    x = x_ref[...] + _rmsnorm(_dot(merged.astype(BF16), wo_ref[...]), gpost_ref[...])
    hf = _rmsnorm(x, gfpre_ref[...]).astype(BF16)
    f = jnp.square(jnp.maximum(_dot(hf, w1_ref[...]), 0.0)).astype(BF16)
    y_ref[...] = x + _rmsnorm(_dot(f, w2_ref[...]), gfpost_ref[...])


def _tail(x, oa, ol, ga, gb, wua, wul, wo, w1, w2, gpost, gfpre, gfpost, *, tm):
    n, d = x.shape
    row = lambda a: pl.BlockSpec((tm, a.shape[1]), lambda i: (i, 0))
    consts = (wua, wul, wo, w1, w2, gpost, gfpre, gfpost)
    return pl.pallas_call(
        _tail_body, grid=(n // tm,),
        in_specs=[row(x), row(oa), row(ol), row(ga), row(gb)] + [_const_spec(c.shape) for c in consts],
        out_specs=row(x), out_shape=jax.ShapeDtypeStruct((n, d), F32),
        compiler_params=_params(1), name="merge_ffn_tail",
    )(x, oa, ol, ga, gb, *consts)


def _row_tile(n, want):
    t = min(want, n)
    while n % t:
        t //= 2
    return t


def _head_pairs_t(a, n_blk, tq):
    heads, _, d = a.shape
    a = a.reshape(heads // 2, 2, n_blk, tq, d).transpose(2, 0, 4, 1, 3)
    return a.reshape(n_blk, heads // 2, d, 2 * tq)


def _layer(xp, xs, ck, cv, cik, page_base, h0, conv0, page_table, w):
    bp, sp, d_model = xp.shape
    nb, t_new, _ = xs.shape
    assert bp == 1, "prompt group is a single sequence"
    assert t_new == SUBLANES, "one sample sequence must fill one 8-row group"
    d_attn = N_HEADS * HEAD_DIM
    d_kv = N_KV_HEADS * HEAD_DIM
    d_lru = w["w_conv"].shape[1]
    dims = dict(d_attn=d_attn, d_kv=d_kv, d_lru=d_lru)

    w_in = w["w_in"]
    o_iq_end = d_attn + 2 * d_kv + IDX_HEADS * IDX_DIM
    o_iw_end = o_iq_end + IDX_HEADS
    o_ik_end = o_iw_end + IDX_DIM
    wa = w_in[:, :o_iq_end].astype(BF16)
    wi = jnp.concatenate([w_in[:, o_iw_end:o_ik_end], w_in[:, o_iq_end:o_iw_end],
                          jnp.zeros((d_model, LANES - IDX_DIM - IDX_HEADS), F32)], axis=1).astype(BF16)
    wl = w_in[:, o_ik_end:].astype(BF16)
    lru_consts = (w["w_conv"], w["b_conv"][None], w["w_rg"].astype(BF16), w["b_rg"][None],
                  w["w_ig"].astype(BF16), w["b_ig"][None], w["lru_lambda"][None])
    tail_consts = (w["w_up_attn"].astype(BF16), w["w_up_lru"].astype(BF16), w["w_out"].astype(BF16),
                   w["w_ff1"].astype(BF16), w["w_ff2"].astype(BF16),
                   w["g_mix_post"][None], w["g_ffn_pre"][None], w["g_ffn_post"][None])
    g_pre = w["g_mix_pre"][None]

    ns = nb * t_new
    xs2 = xs.reshape(ns, d_model)
    tms = _row_tile(ns, 256)
    q, k, v, kb, vb, iq, ikw, xl, xg, ga, gb = _in_projection(xs2, g_pre, wa, wi, wl, tm=tms, **dims)
    prev = jnp.pad(conv0, ((0, 0), (SUBLANES - (CONV_W - 1), 0), (0, 0))).reshape(ns, d_lru)
    h0x = jnp.broadcast_to(h0[:, None, :], (nb, t_new, d_lru)).reshape(ns, d_lru)
    o_lru, h_all = _lru_sample(xl, xg, prev, h0x, *lru_consts, tm=tms, seg=t_new)
    iq_s = iq.reshape(IDX_HEADS, nb, t_new, IDX_DIM).transpose(1, 0, 2, 3).reshape(nb, IDX_HEADS * t_new, IDX_DIM)
    iw_s = ikw[:, IDX_DIM:IDX_DIM + IDX_HEADS].reshape(nb, t_new, IDX_HEADS).transpose(0, 2, 1)
    iw_s = iw_s.reshape(nb, IDX_HEADS * t_new, 1)
    ikn = ikw[:, :IDX_DIM].reshape(nb, t_new, IDX_DIM)
    q_s = q.reshape(nb, t_new, N_KV_HEADS, GROUP, HEAD_DIM).transpose(0, 2, 3, 1, 4)
    q_s = q_s.reshape(nb, N_KV_HEADS, GROUP * t_new, HEAD_DIM)
    n_pages = page_table.shape[1]
    cp = math.gcd(n_pages, 16)
    bias = _sample_select(page_table, iq_s, iw_s, ikn, cik, cp=cp, page_base=page_base,
                          n_seq=math.gcd(nb, 16))
    o_att = _sample_attention(page_table, bias, q_s, k.reshape(nb, t_new, d_kv), v.reshape(nb, t_new, d_kv),
                              ck, cv, page=cik.shape[2], cp=cp, page_base=page_base)
    o_att = o_att.reshape(nb, N_KV_HEADS, GROUP, t_new, HEAD_DIM).transpose(0, 3, 1, 2, 4).reshape(ns, d_attn)
    ys = _tail(xs2, o_att, o_lru, ga, gb, *tail_consts, tm=tms).reshape(nb, t_new, d_model)
    st_s = (k.reshape(nb, t_new, N_KV_HEADS, HEAD_DIM), v.reshape(nb, t_new, N_KV_HEADS, HEAD_DIM), ikn,
            h_all.reshape(nb, t_new, d_lru)[:, -1], xl.reshape(nb, t_new, d_lru)[:, t_new - (CONV_W - 1):])

    xp2 = xp.reshape(sp, d_model)
    tmp = _row_tile(sp, 256)
    q, k, v, kb, vb, iq, ikw, xl, xg, ga, gb = _in_projection(xp2, g_pre, wa, wi, wl, tm=tmp, **dims)
    o_lru, h_last, tail3 = _lru_prompt(xl, xg, *lru_consts, tm=tmp)
    ik = ikw[:, :IDX_DIM]
    tq = _row_tile(sp, LANES)
    q2 = _head_pairs_t(q.reshape(sp, N_HEADS, HEAD_DIM).transpose(1, 0, 2), sp // tq, tq)
    o_att = _prompt_attention(_head_pairs_t(iq, sp // tq, tq), ikw[:, IDX_DIM:IDX_DIM + IDX_HEADS].T, q2,
                              ik.astype(BF16), kb, vb.T, tq=tq, kc=_row_tile(sp // 2, 1024))
    yp = _tail(xp2, o_att, o_lru, ga, gb, *tail_consts, tm=tmp).reshape(1, sp, d_model)
    st_p = (k.reshape(1, sp, N_KV_HEADS, HEAD_DIM), v.reshape(1, sp, N_KV_HEADS, HEAD_DIM),
            ik.reshape(1, sp, IDX_DIM), h_last[SUBLANES - 1:], tail3[None, SUBLANES - (CONV_W - 1):])
    return yp, ys, st_p, st_s


def kernel(x_prompt, x_sample, cache_k, cache_v, cache_idx_k, state_h, state_conv, page_table, w_in, w_conv, b_conv, w_rg, b_rg, w_ig, b_ig, lru_lambda, w_up_attn, w_up_lru, w_out, g_mix_pre, g_mix_post, g_ffn_pre, g_ffn_post, w_ff1, w_ff2):
    names = ("w_in", "w_conv", "b_conv", "w_rg", "b_rg", "w_ig", "b_ig", "lru_lambda", "w_up_attn",
             "w_up_lru", "w_out", "g_mix_pre", "g_mix_post", "g_ffn_pre", "g_ffn_post", "w_ff1", "w_ff2")
    stacked = (w_in, w_conv, b_conv, w_rg, b_rg, w_ig, b_ig, lru_lambda, w_up_attn, w_up_lru, w_out,
               g_mix_pre, g_mix_post, g_ffn_pre, g_ffn_post, w_ff1, w_ff2)
    depth, n_pool, page = cache_idx_k.shape[:3]
    ck = cache_k.reshape(depth * n_pool * page * N_KV_HEADS, HEAD_DIM)
    cv = cache_v.reshape(depth * n_pool * page * N_KV_HEADS, HEAD_DIM)
    cik = jnp.swapaxes(cache_idx_k, 2, 3).reshape(depth * n_pool, IDX_DIM, page)
    yp, ys = x_prompt, x_sample
    new_p, new_s = [], []
    for layer in range(depth):
        w = {n: a[layer] for n, a in zip(names, stacked)}
        yp, ys, st_p, st_s = _layer(yp, ys, ck, cv, cik, layer * n_pool, state_h[layer], state_conv[layer],
                                    page_table, w)
        new_p.append(st_p)
        new_s.append(st_s)
    stack = lambda states, j: jnp.stack([s[j] for s in states], axis=0)
    return (yp, ys) + tuple(stack(new_p, j) for j in range(5)) + tuple(stack(new_s, j) for j in range(5))
```

```python
import functools
import math

import jax
import jax.numpy as jnp
from jax import lax
from jax.experimental import pallas as pl
from jax.experimental.pallas import tpu as pltpu

N_HEADS = 8
HEAD_DIM = 128
N_KV_HEADS = 2
GROUP = N_HEADS // N_KV_HEADS
IDX_HEADS = 8
IDX_DIM = 64
TOPK_MAX = 256
LRU_BLOCKS = 8
CONV_W = 4
LRU_C = 8.0
EPS = 1e-6

LANES = 128
SUBLANES = 8
VMEM_LIMIT = 56 * 1024 * 1024

F32 = jnp.float32
BF16 = jnp.bfloat16
NEG_BIG = -1e30
IDX_BIG = 1e9
LOG2E = math.log2(math.e)


def _const_spec(shape):
    nd = len(shape)
    return pl.BlockSpec(shape, lambda *_: (0,) * nd, pipeline_mode=pl.Buffered(1))


def _params(n_grid):
    return pltpu.CompilerParams(dimension_semantics=("arbitrary",) * n_grid,
                                vmem_limit_bytes=VMEM_LIMIT)


def _rmsnorm(x, g):
    return x * lax.rsqrt(jnp.mean(x * x, axis=-1, keepdims=True) + EPS) * g


def _dot(a, b):
    return jnp.dot(a, b, preferred_element_type=F32)


def _dot_nt(a, b):
    return lax.dot_general(a, b, (((1,), (1,)), ((), ())), preferred_element_type=F32)


def _inproj_body(x_ref, g_ref, wa_ref, wi_ref, wl_ref,
                 q_ref, k_ref, v_ref, kb_ref, vb_ref, iq_ref, ikw_ref, xl_ref, xg_ref, ga_ref, gb_ref,
                 *, d_attn, d_kv, d_lru, d_model):
    hn = _rmsnorm(x_ref[...], g_ref[...]).astype(BF16)
    q_ref[...] = _dot(hn, wa_ref[:, :d_attn]).astype(BF16)
    k = _dot(hn, wa_ref[:, d_attn:d_attn + d_kv])
    k_ref[...] = k
    kb_ref[...] = k.astype(BF16)
    v = _dot(hn, wa_ref[:, d_attn + d_kv:d_attn + 2 * d_kv])
    v_ref[...] = v
    vb_ref[...] = v.astype(BF16)
    iq = _dot(hn, wa_ref[:, d_attn + 2 * d_kv:])
    for h in range(IDX_HEADS):
        iq_ref[h] = iq[:, h * IDX_DIM:(h + 1) * IDX_DIM].astype(BF16)
    ikw_ref[...] = _dot(hn, wi_ref[...])
    xl_ref[...] = _dot(hn, wl_ref[:, :d_lru])
    xg_ref[...] = _dot(hn, wl_ref[:, d_lru:2 * d_lru])
    ga_ref[...] = _dot(hn, wl_ref[:, 2 * d_lru:2 * d_lru + d_model])
    gb_ref[...] = _dot(hn, wl_ref[:, 2 * d_lru + d_model:])


def _in_projection(x, g, wa, wi, wl, *, d_attn, d_kv, d_lru, tm):
    n, d_model = x.shape
    grid = (n // tm,)
    row = lambda w: pl.BlockSpec((tm, w), lambda i: (i, 0))
    out_shape = (
        jax.ShapeDtypeStruct((n, d_attn), BF16),
        jax.ShapeDtypeStruct((n, d_kv), F32),
        jax.ShapeDtypeStruct((n, d_kv), F32),
        jax.ShapeDtypeStruct((n, d_kv), BF16),
        jax.ShapeDtypeStruct((n, d_kv), BF16),
        jax.ShapeDtypeStruct((IDX_HEADS, n, IDX_DIM), BF16),
        jax.ShapeDtypeStruct((n, LANES), F32),
        jax.ShapeDtypeStruct((n, d_lru), F32),
        jax.ShapeDtypeStruct((n, d_lru), F32),
        jax.ShapeDtypeStruct((n, d_model), F32),
        jax.ShapeDtypeStruct((n, d_model), F32),
    )
    out_specs = (row(d_attn), row(d_kv), row(d_kv), row(d_kv), row(d_kv),
                 pl.BlockSpec((IDX_HEADS, tm, IDX_DIM), lambda i: (0, i, 0)),
                 row(LANES), row(d_lru), row(d_lru), row(d_model), row(d_model))
    body = functools.partial(_inproj_body, d_attn=d_attn, d_kv=d_kv, d_lru=d_lru, d_model=d_model)
    return pl.pallas_call(
        body, grid=grid,
        in_specs=[row(d_model), _const_spec(g.shape), _const_spec(wa.shape), _const_spec(wi.shape),
                  _const_spec(wl.shape)],
        out_specs=out_specs, out_shape=out_shape, compiler_params=_params(1), name="in_projection",
    )(x, g, wa, wi, wl)


def _shift_rows(cur, prev, j, row_in_seg):
    tm = cur.shape[0]
    from_prev = pltpu.roll(prev, tm - SUBLANES + j, axis=0)
    return jnp.where(row_in_seg < j, from_prev, pltpu.roll(cur, j, axis=0))


def _lru_core(xl, xg, prev, h_in, wc, bc, wrg, brg, wig, big, lam, *, seg):
    tm, d = xl.shape
    row = lax.broadcasted_iota(jnp.int32, (tm, 1), 0)
    row_in_seg = row % seg
    u = bc + wc[CONV_W - 1:CONV_W] * xl
    for j in range(1, CONV_W):
        u = u + wc[CONV_W - 1 - j:CONV_W - j] * _shift_rows(xl, prev, j, row_in_seg)
    ub = u.astype(BF16)
    blk = d // LRU_BLOCKS
    r = jnp.concatenate([_dot(ub[:, n * blk:(n + 1) * blk], wrg[n]) for n in range(LRU_BLOCKS)], axis=1)
    ig = jnp.concatenate([_dot(ub[:, n * blk:(n + 1) * blk], wig[n]) for n in range(LRU_BLOCKS)], axis=1)
    r = jax.nn.sigmoid(r + brg)
    ig = jax.nn.sigmoid(ig + big)
    softplus_neg = jnp.maximum(-lam, 0.0) + jnp.log1p(jnp.exp(-jnp.abs(lam)))
    log_a = -LRU_C * r * softplus_neg
    a = jnp.exp(log_a)
    b = jnp.sqrt(-jnp.tanh(log_a) * (a * a + 1.0)) * (ig * u)
    d_step = 1
    while d_step < seg:
        keep = row_in_seg >= d_step
        a_sh = jnp.where(keep, pltpu.roll(a, d_step, axis=0), 1.0)
        b_sh = jnp.where(keep, pltpu.roll(b, d_step, axis=0), 0.0)
        b = a * b_sh + b
        a = a * a_sh
        d_step *= 2
    h = a * h_in + b
    gelu = 0.5 * xg * (1.0 + jnp.tanh(math.sqrt(2.0 / math.pi) * (xg + 0.044715 * (xg * xg * xg))))
    return h * gelu, h


def _lru_prompt_body(xl_ref, xg_ref, wc_ref, bc_ref, wrg_ref, brg_ref, wig_ref, big_ref, lam_ref,
                     o_ref, h_ref, tail_ref, hc_sc, prev_sc):
    @pl.when(pl.program_id(0) == 0)
    def _():
        hc_sc[...] = jnp.zeros_like(hc_sc)
        prev_sc[...] = jnp.zeros_like(prev_sc)

    xl = xl_ref[...]
    tm = xl.shape[0]
    prev = jnp.tile(prev_sc[...], (tm // SUBLANES, 1))
    out, h = _lru_core(xl, xg_ref[...], prev, hc_sc[0:1, :], wc_ref[...], bc_ref[...], wrg_ref[...],
                       brg_ref[...], wig_ref[...], big_ref[...], lam_ref[...], seg=tm)
    o_ref[...] = out.astype(BF16)
    last = h[tm - SUBLANES:, :]
    hc_sc[...] = jnp.broadcast_to(last[SUBLANES - 1:SUBLANES, :], hc_sc.shape)
    h_ref[...] = last
    prev_sc[...] = xl[tm - SUBLANES:, :]
    tail_ref[...] = xl[tm - SUBLANES:, :]


def _lru_prompt(xl, xg, wc, bc, wrg, brg, wig, big, lam, *, tm):
    n, d = xl.shape
    row = pl.BlockSpec((tm, d), lambda i: (i, 0))
    last8 = pl.BlockSpec((SUBLANES, d), lambda i: (0, 0))
    consts = (wc, bc, wrg, brg, wig, big, lam)
    return pl.pallas_call(
        _lru_prompt_body, grid=(n // tm,),
        in_specs=[row, row] + [_const_spec(c.shape) for c in consts],
        out_specs=(row, last8, last8),
        out_shape=(jax.ShapeDtypeStruct((n, d), BF16),
                   jax.ShapeDtypeStruct((SUBLANES, d), F32),
                   jax.ShapeDtypeStruct((SUBLANES, d), F32)),
        scratch_shapes=[pltpu.VMEM((SUBLANES, d), F32), pltpu.VMEM((SUBLANES, d), F32)],
        compiler_params=_params(1), name="lru_prompt",
    )(xl, xg, *consts)


def _lru_sample_body(xl_ref, xg_ref, prev_ref, h0_ref, wc_ref, bc_ref, wrg_ref, brg_ref, wig_ref,
                     big_ref, lam_ref, o_ref, h_ref, *, seg):
    out, h = _lru_core(xl_ref[...], xg_ref[...], prev_ref[...], h0_ref[...], wc_ref[...], bc_ref[...],
                       wrg_ref[...], brg_ref[...], wig_ref[...], big_ref[...], lam_ref[...], seg=seg)
    o_ref[...] = out.astype(BF16)
    h_ref[...] = h


def _lru_sample(xl, xg, prev, h0, wc, bc, wrg, brg, wig, big, lam, *, tm, seg):
    n, d = xl.shape
    row = pl.BlockSpec((tm, d), lambda i: (i, 0))
    consts = (wc, bc, wrg, brg, wig, big, lam)
    return pl.pallas_call(
        functools.partial(_lru_sample_body, seg=seg), grid=(n // tm,),
        in_specs=[row, row, row, row] + [_const_spec(c.shape) for c in consts],
        out_specs=(row, row),
        out_shape=(jax.ShapeDtypeStruct((n, d), BF16), jax.ShapeDtypeStruct((n, d), F32)),
        compiler_params=_params(1), name="lru_sample",
    )(xl, xg, prev, h0, *consts)


def _ordered_bits_to_float(u):
    key = u ^ jnp.int32(-2 ** 31)
    bits = jnp.where(key >= 0, key, key ^ jnp.int32(0x7FFFFFFF))
    return lax.bitcast_convert_type(bits, F32)


REFINE_STEPS = 12


def _float_to_ordered_bits(x):
    bits = lax.bitcast_convert_type(x, jnp.int32)
    key = jnp.where(bits >= 0, bits, bits ^ jnp.int32(0x7FFFFFFF))
    return key ^ jnp.int32(-2 ** 31)


def _kth_search(count_ge, k_eff, bracket=None):
    shape = k_eff.shape
    zero = jnp.zeros(shape, jnp.int32)
    if bracket is None:
        i0, u0 = jnp.int32(0), zero
    else:
        u_lo, u_hi = (_float_to_ordered_bits(x) for x in bracket)
        i0 = jnp.minimum(jnp.min(lax.clz(u_lo ^ u_hi)), 31)
        keep = lax.shift_left(lax.shift_left(jnp.int32(-1), 31 - i0), 1)
        u0 = u_hi & keep

    def cond(st):
        return (st[0] < 32) & (st[4] == 0)

    def body(st):
        i, u, thr, exact, _ = st
        cand = u | lax.shift_left(jnp.int32(1), 31 - i)
        t = _ordered_bits_to_float(cand)
        c = count_ge(t)
        hit = (c == k_eff) & (exact == 0)
        thr = jnp.where(hit, t, thr)
        exact = jnp.where(hit, 1, exact)
        u = jnp.where(c >= k_eff, cand, u)
        return i + 1, u, thr, exact, jnp.min(exact)

    init = (i0, u0, jnp.zeros(shape, F32), zero, jnp.int32(0))
    _, u, thr, exact, all_exact = lax.while_loop(cond, body, init)

    def refine_cond(st):
        return (st[0] < REFINE_STEPS) & (st[5] == 0)

    def refine(st):
        j, lo, hi, thr, exact, _ = st
        mid = lo + (hi - lo) * 0.5
        c = count_ge(mid)
        hit = (c == k_eff) & (exact == 0)
        thr = jnp.where(hit, mid, thr)
        exact = jnp.where(hit, 1, exact)
        return j + 1, jnp.where(c > k_eff, mid, lo), jnp.where(c < k_eff, mid, hi), thr, exact, jnp.min(exact)

    init = (jnp.int32(0), _ordered_bits_to_float(u), _ordered_bits_to_float(u + 1), thr, exact, all_exact)
    _, lo, _, thr, exact, _ = lax.while_loop(refine_cond, refine, init)
    return jnp.where(exact == 1, thr, lo), exact


BRACKET_BITS = 24
BRACKET_BLIND_STEPS = 14


def _bracket_search(count_ge, k_eff, bracket):
    lo, hi = bracket
    lo = jnp.maximum(lo, NEG_BIG)
    width = hi - lo
    shape = k_eff.shape
    zero = jnp.zeros(shape, jnp.int32)

    def cond(st):
        return (st[0] < BRACKET_BITS) & (st[4] == 0)

    def halve(i, f, thr, exact):
        cand = f | lax.shift_left(jnp.int32(1), BRACKET_BITS - 1 - i)
        t = lo + (cand.astype(F32) * 2.0 ** -BRACKET_BITS) * width
        c = count_ge(t)
        hit = (c == k_eff) & (exact == 0)
        return jnp.where(c >= k_eff, cand, f), jnp.where(hit, t, thr), jnp.where(hit, 1, exact)

    def body(st):
        i, f, thr, exact, _ = st
        f, thr, exact = halve(i, f, thr, exact)
        return i + 1, f, thr, exact, jnp.min(exact)

    f, thr, exact = lax.fori_loop(0, BRACKET_BLIND_STEPS, lambda i, st: halve(i, *st),
                                  (zero, jnp.zeros(shape, F32), zero))
    init = (jnp.int32(BRACKET_BLIND_STEPS), f, thr, exact, jnp.min(exact))
    _, f, thr, exact, _ = lax.while_loop(cond, body, init)
    top = f == 2 ** BRACKET_BITS - 1
    t_above = jnp.where(top, jnp.inf, lo + ((f + 1).astype(F32) * 2.0 ** -BRACKET_BITS) * width)
    return thr, exact, t_above


def _tie_cut(count_lt, k_eff, n_bits):
    def step(i, j):
        cand = j | lax.shift_left(jnp.int32(1), n_bits - 1 - i)
        ok = count_lt(cand.astype(F32)) < k_eff
        return jnp.where(ok, cand, j)
    j = lax.fori_loop(0, n_bits, step, jnp.zeros(k_eff.shape, jnp.int32))
    return j.astype(F32)


def _rank(score, thr, key_idx):
    return jnp.where(score > thr, -1.0, jnp.where(score == thr, key_idx, IDX_BIG))


def _tree_sum(terms):
    while len(terms) > 1:
        terms = [a + b for a, b in zip(terms[0::2], terms[1::2])] + ([terms[-1]] if len(terms) % 2 else [])
    return terms[0]


COUNT_ROWS = 64


def _pattn_body(iq2_ref, iwt_ref, q2_ref, ik_ref, k_ref, vt_ref, o_ref,
                s_sc, cm_sc, sa_sc, sb_sc, pa_sc, pb_sc, thr_sc, m_sc, l_sc, acc_sc,
                *, tq, kc, n_sel, n_bits):
    qb = pl.program_id(0)
    n_chunks = ((qb + 1) * tq + kc - 1) // kc
    n_chunks_even = (n_chunks + 1) // 2 * 2
    q_pos = qb * tq + lax.broadcasted_iota(jnp.int32, (1, tq), 1)
    n_pairs = N_HEADS // 2
    iw = iwt_ref[...] * (IDX_DIM ** -0.5 * IDX_HEADS ** -0.5)

    def chunk_off(c):
        return pl.multiple_of(c * kc, kc)

    def score_chunk(c, carry):
        off = chunk_off(c)
        ikc = ik_ref[pl.ds(off, kc), :]
        acc = jnp.zeros((kc, tq), F32)
        for hp in range(IDX_HEADS // 2):
            lg = _dot(ikc, iq2_ref[0, hp])
            acc = acc + jnp.maximum(lg[:, :tq], 0.0) * iw[2 * hp:2 * hp + 1, :]
            acc = acc + jnp.maximum(lg[:, tq:], 0.0) * iw[2 * hp + 1:2 * hp + 2, :]
        k_pos = off + lax.broadcasted_iota(jnp.int32, (kc, 1), 0)
        sc = jnp.where(k_pos <= q_pos, acc, -jnp.inf)
        s_sc[pl.ds(off, kc), :] = sc
        cm_sc[...] = functools.reduce(jnp.maximum, [cm_sc[...]] + [
            sc[j * TOPK_MAX:(j + 1) * TOPK_MAX] for j in range(kc // TOPK_MAX)])
        return carry
    cm_sc[...] = jnp.full_like(cm_sc, -jnp.inf)
    lax.fori_loop(0, n_chunks_even, score_chunk, 0)
    bracket = (jnp.min(cm_sc[...], axis=0, keepdims=True), jnp.max(cm_sc[...], axis=0, keepdims=True))

    def count_where(pred):
        def body(c, acc):
            off = chunk_off(c)
            for j in range(kc // COUNT_ROWS):
                acc = acc + jnp.where(pred(s_sc[pl.ds(off + j * COUNT_ROWS, COUNT_ROWS), :]), 1, 0)
            return acc
        acc = lax.fori_loop(0, n_chunks, body, jnp.zeros((COUNT_ROWS, tq), jnp.int32))
        return jnp.sum(acc, axis=0, keepdims=True)

    k_eff = jnp.minimum(q_pos + 1, n_sel)
    def max_below(t):
        def body(c, acc):
            off = chunk_off(c)
            for j in range(kc // COUNT_ROWS):
                blk = s_sc[pl.ds(off + j * COUNT_ROWS, COUNT_ROWS), :]
                acc = jnp.maximum(acc, jnp.where(blk < t, blk, -jnp.inf))
            return acc
        acc = lax.fori_loop(0, n_chunks, body, jnp.full((COUNT_ROWS, tq), -jnp.inf, F32))
        return jnp.max(acc, axis=0, keepdims=True)

    def break_ties(thr):
        def rank_chunk(c, carry):
            off = chunk_off(c)
            key_idx = (off + lax.broadcasted_iota(jnp.int32, (kc, 1), 0)).astype(F32)
            s_sc[pl.ds(off, kc), :] = _rank(s_sc[pl.ds(off, kc), :], thr, key_idx)
            return carry
        lax.fori_loop(0, n_chunks, rank_chunk, 0)
        cut = _tie_cut(lambda j: count_where(lambda blk: blk < j), k_eff, n_bits)

        def recode_chunk(c, carry):
            off = chunk_off(c)
            s_sc[pl.ds(off, kc), :] = jnp.where(s_sc[pl.ds(off, kc), :] <= cut, 1.0, -1.0)
            return carry
        lax.fori_loop(0, n_chunks, recode_chunk, 0)
        thr_sc[...] = jnp.zeros_like(thr_sc)

    count_ge = lambda t: count_where(lambda blk: blk >= t)
    thr_b, exact_b, t_above = _bracket_search(count_ge, k_eff, bracket)
    thr_sc[...] = jnp.broadcast_to(thr_b, thr_sc.shape)

    @pl.when(jnp.min(exact_b) == 0)
    def _():
        v = max_below(t_above)
        n_ge = count_ge(v)
        n_gt = count_where(lambda blk: blk > v)
        found = (exact_b == 1) | ((n_gt < k_eff) & (n_ge >= k_eff))
        thr_v = jnp.where(exact_b == 1, thr_b, v)
        tied = (exact_b == 0) & (n_ge > k_eff)
        thr_sc[...] = jnp.broadcast_to(thr_v, thr_sc.shape)
        all_found = jnp.min(found.astype(jnp.int32))

        @pl.when((all_found == 1) & (jnp.max(tied.astype(jnp.int32)) == 1))
        def _():
            break_ties(thr_v)

        @pl.when(all_found == 0)
        def _():
            thr, exact = _kth_search(count_ge, k_eff, bracket)
            thr_sc[...] = jnp.broadcast_to(thr, thr_sc.shape)

            @pl.when(jnp.min(exact) == 0)
            def _():
                break_ties(thr)

    sel_thr = thr_sc[0:1, :]
    m_sc[...] = jnp.full_like(m_sc, NEG_BIG)
    l_sc[...] = jnp.zeros_like(l_sc)
    acc_sc[...] = jnp.zeros_like(acc_sc)
    c2 = HEAD_DIM ** -0.5 * LOG2E
    last_chunk = k_ref.shape[0] // kc - 1

    def qk(c, s_buf):
        off = chunk_off(jnp.minimum(c, last_chunk))
        for hp in range(n_pairs):
            g = (2 * hp) // GROUP
            s_buf[:, 2 * hp * tq:(2 * hp + 2) * tq] = _dot(
                k_ref[pl.ds(off, kc), g * HEAD_DIM:(g + 1) * HEAD_DIM], q2_ref[0, hp])

    def pv(c, p_buf, alphas):
        off = chunk_off(jnp.maximum(c, 0))
        for hp in range(n_pairs):
            g = (2 * hp) // GROUP
            acc_sc[hp] = alphas[hp] * acc_sc[hp] + _dot(
                vt_ref[g * HEAD_DIM:(g + 1) * HEAD_DIM, pl.ds(off, kc)],
                p_buf[:, 2 * hp * tq:(2 * hp + 2) * tq])

    def softmax(c, s_buf, p_buf):
        bias = jnp.where(s_sc[pl.ds(chunk_off(c), kc), :] >= sel_thr, 0.0, NEG_BIG)
        alphas = []
        for h in range(N_HEADS):
            s = s_buf[:, h * tq:(h + 1) * tq] + bias
            s_buf[:, h * tq:(h + 1) * tq] = s
            m_old = m_sc[h]
            m_new = jnp.maximum(m_old, jnp.max(s, axis=0, keepdims=True))
            alpha = jnp.exp2((m_old - m_new) * c2)
            p = jnp.exp2((s_buf[:, h * tq:(h + 1) * tq] - m_new[0:1, :]) * c2)
            l_sc[h] = alpha * l_sc[h] + jnp.sum(p, axis=0, keepdims=True)
            m_sc[h] = m_new
            p_buf[:, h * tq:(h + 1) * tq] = p.astype(BF16)
            alphas.append(alpha[0:1, :])
        return [jnp.concatenate(alphas[2 * hp:2 * hp + 2], axis=1) for hp in range(n_pairs)]

    def stage(c, s_cur, p_cur, s_next, p_prev, alphas_prev):
        qk(c + 1, s_next)
        pv(c - 1, p_prev, alphas_prev)
        return softmax(c, s_cur, p_cur)

    qk(0, sa_sc)
    pb_sc[...] = jnp.zeros_like(pb_sc)

    def attn_two_chunks(i, alphas):
        alphas = stage(2 * i, sa_sc, pa_sc, sb_sc, pb_sc, alphas)
        return stage(2 * i + 1, sb_sc, pb_sc, sa_sc, pa_sc, alphas)
    ones = [jnp.ones((1, 2 * tq), F32)] * n_pairs
    alphas = lax.fori_loop(0, n_chunks_even // 2, attn_two_chunks, ones)
    pv(n_chunks_even - 1, pb_sc, alphas)

    for h in range(N_HEADS):
        o_t = acc_sc[h // 2][:, (h % 2) * tq:(h % 2 + 1) * tq] / l_sc[h][0:1, :]
        o_ref[:, h * HEAD_DIM:(h + 1) * HEAD_DIM] = o_t.T.astype(BF16)


def _prompt_attention(iq2, iwt, q2, ik, kb, vt, *, tq, kc):
    n = kb.shape[0]
    d_attn = N_HEADS * HEAD_DIM
    n_sel = min(TOPK_MAX, n // 4)
    n_bits = max(1, (n - 1).bit_length())
    assert (n // kc) % 2 == 0, "the attention pipeline consumes key chunks in pairs"
    assert kc % TOPK_MAX == 0, "a key chunk holds whole sets of residue classes"
    body = functools.partial(_pattn_body, tq=tq, kc=kc, n_sel=n_sel, n_bits=n_bits)
    blk4 = lambda a: pl.BlockSpec((1,) + a.shape[1:], lambda i: (i, 0, 0, 0))
    return pl.pallas_call(
        body, grid=(n // tq,),
        in_specs=[blk4(iq2), pl.BlockSpec((IDX_HEADS, tq), lambda i: (0, i)), blk4(q2),
                  _const_spec(ik.shape), _const_spec(kb.shape), _const_spec(vt.shape)],
        out_specs=pl.BlockSpec((tq, d_attn), lambda i: (i, 0)),
        out_shape=jax.ShapeDtypeStruct((n, d_attn), BF16),
        scratch_shapes=[pltpu.VMEM((n, tq), F32),
                        pltpu.VMEM((TOPK_MAX, tq), F32),
                        pltpu.VMEM((kc, N_HEADS * tq), F32),
                        pltpu.VMEM((kc, N_HEADS * tq), F32),
                        pltpu.VMEM((kc, N_HEADS * tq), BF16),
                        pltpu.VMEM((kc, N_HEADS * tq), BF16),
                        pltpu.VMEM((SUBLANES, tq), F32),
                        pltpu.VMEM((N_HEADS, SUBLANES, tq), F32),
                        pltpu.VMEM((N_HEADS, SUBLANES, tq), F32),
                        pltpu.VMEM((N_HEADS // 2, HEAD_DIM, 2 * tq), F32)],
        compiler_params=_params(1), name="prompt_attention",
    )(iq2, iwt, q2, ik, kb, vt)


def _ssel_body(pt_ref, iq_ref, iw_ref, ikn_ref, cik_hbm, o_ref,
               ik_buf, s_sc, snew_sc, cm_sc, thr_sc, ik_sem,
               *, n_pages, page, cp, t_new, n_sel, n_bits, page_base, n_seq):
    grp = pl.program_id(0)
    nb = pl.num_programs(0) * n_seq
    past = n_pages * page
    n_ch = n_pages // cp
    kc = cp * page
    rows = n_seq * t_new

    def start_ik(seq, slot):
        def body(p, c):
            pltpu.make_async_copy(cik_hbm.at[pt_ref[seq * n_pages + p] + page_base],
                                  ik_buf.at[slot, :, pl.ds(p * page, page)], ik_sem.at[slot]).start()
            return c
        lax.fori_loop(0, n_pages, body, 0)

    def wait_ik(slot):
        pltpu.make_async_copy(ik_buf.at[slot], ik_buf.at[slot], ik_sem.at[slot]).wait()

    @pl.when(grp == 0)
    def _():
        start_ik(0, 0)

    t_idx = lax.broadcasted_iota(jnp.int32, (t_new, LANES), 0)
    j_idx = lax.broadcasted_iota(jnp.int32, (t_new, LANES), 1)

    def head_sum(x):
        return _tree_sum([x[h * t_new:(h + 1) * t_new] for h in range(IDX_HEADS)])

    def score_seq(i, carry):
        seq = grp * n_seq + i
        slot = seq % 2

        @pl.when(seq + 1 < nb)
        def _():
            start_ik(seq + 1, 1 - slot)

        wait_ik(slot)
        iq = iq_ref[i]
        iw = iw_ref[i] * (IDX_DIM ** -0.5 * IDX_HEADS ** -0.5)
        r0 = pl.multiple_of(i * t_new, t_new)
        cm = [jnp.full((t_new, LANES), -jnp.inf, F32)] * 2
        for c in range(n_ch):
            ikc = ik_buf[slot, :, c * kc:(c + 1) * kc].astype(BF16)
            sc = head_sum(jnp.maximum(_dot(iq, ikc), 0.0) * iw)
            s_sc[pl.ds(r0, t_new), c * kc:(c + 1) * kc] = sc
            for j in range(kc // LANES):
                cm[j % 2] = jnp.maximum(cm[j % 2], sc[:, j * LANES:(j + 1) * LANES])
        cm_sc[pl.ds(r0, t_new), :] = jnp.concatenate(cm, axis=1)
        ikn = jnp.concatenate([ikn_ref[i], jnp.zeros((LANES - t_new, IDX_DIM), F32)], axis=0)
        s_new = head_sum(jnp.maximum(_dot_nt(iq, ikn.astype(BF16)), 0.0) * iw)
        snew_sc[pl.ds(r0, t_new), :] = jnp.where(j_idx <= t_idx, s_new, -jnp.inf)
        return carry
    lax.fori_loop(0, n_seq, score_seq, 0)

    def count_where(pred):
        terms = [jnp.where(pred(snew_sc[...]), 1, 0)]
        for j in range(past // LANES):
            terms.append(jnp.where(pred(s_sc[:, j * LANES:(j + 1) * LANES]), 1, 0))
        return jnp.sum(_tree_sum(terms), axis=1, keepdims=True)

    k_eff = jnp.full((rows, 1), n_sel, jnp.int32)
    assert n_sel <= 2 * LANES <= past
    lo = jnp.min(cm_sc[...], axis=1, keepdims=True)
    hi = jnp.maximum(jnp.max(cm_sc[...], axis=1, keepdims=True), jnp.max(snew_sc[...], axis=1, keepdims=True))
    count_ge = lambda t: count_where(lambda blk: blk >= t)
    thr_b, exact_b, _ = _bracket_search(count_ge, k_eff, (lo, hi))
    thr_sc[...] = jnp.broadcast_to(thr_b, thr_sc.shape)

    @pl.when(jnp.min(exact_b) == 0)
    def _():
        thr, exact = _kth_search(count_ge, k_eff, (lo, hi))
        thr_sc[...] = jnp.broadcast_to(thr, thr_sc.shape)

        @pl.when(jnp.min(exact) == 0)
        def _():
            lane_f = lax.broadcasted_iota(jnp.int32, (1, kc), 1)
            j_all = lax.broadcasted_iota(jnp.int32, (1, LANES), 1)
            for c in range(n_ch):
                s_sc[:, c * kc:(c + 1) * kc] = _rank(s_sc[:, c * kc:(c + 1) * kc], thr,
                                                     (c * kc + lane_f).astype(F32))
            snew_sc[...] = _rank(snew_sc[...], thr, (past + j_all).astype(F32))
            cut = _tie_cut(lambda j: count_where(lambda blk: blk < j), k_eff, n_bits)
            for c in range(n_ch):
                s_sc[:, c * kc:(c + 1) * kc] = jnp.where(s_sc[:, c * kc:(c + 1) * kc] <= cut, 1.0, -1.0)
            snew_sc[...] = jnp.where(snew_sc[...] <= cut, 1.0, -1.0)
            thr_sc[...] = jnp.zeros_like(thr_sc)

    sel_thr = thr_sc[:, 0:1]
    for c in range(n_ch):
        o_ref[:, c * kc:(c + 1) * kc] = jnp.where(s_sc[:, c * kc:(c + 1) * kc] >= sel_thr, 0.0, NEG_BIG)
    o_ref[:, past:] = jnp.where(snew_sc[...] >= sel_thr, 0.0, NEG_BIG)


def _sample_select(page_table, iq, iw, ikn, cik, *, cp, page_base, n_seq):
    nb, n_pages = page_table.shape
    page = cik.shape[2]
    t_new = ikn.shape[1]
    past = n_pages * page
    n_sel = min(TOPK_MAX, (past + t_new) // 4)
    n_bits = max(1, (past + t_new - 1).bit_length())
    rows = n_seq * t_new
    body = functools.partial(_ssel_body, n_pages=n_pages, page=page, cp=cp, t_new=t_new,
                             n_sel=n_sel, n_bits=n_bits, page_base=page_base, n_seq=n_seq)
    grp3 = lambda a: pl.BlockSpec((n_seq,) + a.shape[1:], lambda i, pt: (i, 0, 0))
    grid_spec = pltpu.PrefetchScalarGridSpec(
        num_scalar_prefetch=1, grid=(nb // n_seq,),
        in_specs=[grp3(iq), grp3(iw), grp3(ikn), pl.BlockSpec(memory_space=pl.ANY)],
        out_specs=pl.BlockSpec((rows, past + LANES), lambda i, pt: (i, 0)),
        scratch_shapes=[pltpu.VMEM((2, IDX_DIM, past), F32),
                        pltpu.VMEM((rows, past), F32),
                        pltpu.VMEM((rows, LANES), F32),
                        pltpu.VMEM((rows, 2 * LANES), F32),
                        pltpu.VMEM((rows, LANES), F32),
                        pltpu.SemaphoreType.DMA((2,))],
    )
    return pl.pallas_call(
        body, grid_spec=grid_spec,
        out_shape=jax.ShapeDtypeStruct((nb * t_new, past + LANES), F32),
        compiler_params=_params(1), name="sample_select",
    )(page_table.reshape(-1), iq, iw, ikn, cik)


def _sattn_body(pt_ref, bias_ref, q_ref, kn_ref, vn_ref, ck_hbm, cv_hbm,
                o_ref, k_buf, v_buf, k_sem, v_sem, *, n_pages, page, cp, t_new, page_base):
    b = pl.program_id(0)
    nb = pl.num_programs(0)
    past = n_pages * page
    n_ch = n_pages // cp
    kc = cp * page
    page_rows = page * N_KV_HEADS

    def start_kv(seq, chunk, slot):
        def body(p, c):
            src = pl.ds((pt_ref[seq * n_pages + chunk * cp + p] + page_base) * page_rows, page_rows)
            dst = pl.ds(p * page_rows, page_rows)
            pltpu.make_async_copy(ck_hbm.at[src, :], k_buf.at[slot, dst, :], k_sem.at[slot]).start()
            pltpu.make_async_copy(cv_hbm.at[src, :], v_buf.at[slot, dst, :], v_sem.at[slot]).start()
            return c
        lax.fori_loop(0, cp, body, 0)

    def wait_kv(slot):
        pltpu.make_async_copy(k_buf.at[slot], k_buf.at[slot], k_sem.at[slot]).wait()
        pltpu.make_async_copy(v_buf.at[slot], v_buf.at[slot], v_sem.at[slot]).wait()

    @pl.when(b == 0)
    def _():
        start_kv(0, 0, 0)

    def pad_keys(x):
        return jnp.concatenate([x, jnp.zeros((LANES - t_new, x.shape[1]), F32)], axis=0).astype(BF16)

    c2 = HEAD_DIM ** -0.5 * LOG2E
    bias_new = jnp.tile(bias_ref[:, past:], (GROUP, 1))
    m = []
    l = []
    acc = []
    for g in range(N_KV_HEADS):
        kn = pad_keys(kn_ref[0, :, g * HEAD_DIM:(g + 1) * HEAD_DIM])
        vn = pad_keys(vn_ref[0, :, g * HEAD_DIM:(g + 1) * HEAD_DIM])
        s = _dot_nt(q_ref[0, g], kn) + bias_new
        m_g = jnp.max(s, axis=1, keepdims=True)
        p = jnp.exp2((s - m_g) * c2)
        m.append(m_g)
        l.append(jnp.sum(p, axis=1, keepdims=True))
        acc.append(_dot(p.astype(BF16), vn))

    for c in range(n_ch):
        slot = (b * n_ch + c) % 2
        if c + 1 < n_ch:
            start_kv(b, c + 1, 1 - slot)
        else:
            @pl.when(b + 1 < nb)
            def _():
                start_kv(b + 1, 0, 1 - slot)
        wait_kv(slot)
        bias = jnp.tile(bias_ref[:, c * kc:(c + 1) * kc], (GROUP, 1))
        for g in range(N_KV_HEADS):
            kg = k_buf[slot, pl.ds(g, kc, stride=N_KV_HEADS), :].astype(BF16)
            vg = v_buf[slot, pl.ds(g, kc, stride=N_KV_HEADS), :].astype(BF16)
            s = _dot_nt(q_ref[0, g], kg) + bias
            m_new = jnp.maximum(m[g], jnp.max(s, axis=1, keepdims=True))
            alpha = jnp.exp2((m[g] - m_new) * c2)
            p = jnp.exp2((s - m_new) * c2)
            l[g] = alpha * l[g] + jnp.sum(p, axis=1, keepdims=True)
            acc[g] = alpha * acc[g] + _dot(p.astype(BF16), vg)
            m[g] = m_new

    for g in range(N_KV_HEADS):
        o_ref[0, g] = (acc[g] / l[g]).astype(BF16)


def _sample_attention(page_table, bias, q, kn, vn, ck, cv, *, page, cp, page_base):
    nb, n_pages = page_table.shape
    t_new = kn.shape[1]
    rows = GROUP * t_new
    body = functools.partial(_sattn_body, n_pages=n_pages, page=page, cp=cp, t_new=t_new,
                             page_base=page_base)
    seq3 = lambda a: pl.BlockSpec((1,) + a.shape[1:], lambda i, pt: (i, 0, 0))
    seq4 = lambda a: pl.BlockSpec((1,) + a.shape[1:], lambda i, pt: (i, 0, 0, 0))
    any_spec = pl.BlockSpec(memory_space=pl.ANY)
    grid_spec = pltpu.PrefetchScalarGridSpec(
        num_scalar_prefetch=1, grid=(nb,),
        in_specs=[pl.BlockSpec((t_new, bias.shape[1]), lambda i, pt: (i, 0)), seq4(q), seq3(kn), seq3(vn),
                  any_spec, any_spec],
        out_specs=pl.BlockSpec((1, N_KV_HEADS, rows, HEAD_DIM), lambda i, pt: (i, 0, 0, 0)),
        scratch_shapes=[pltpu.VMEM((2, cp * page * N_KV_HEADS, HEAD_DIM), F32),
                        pltpu.VMEM((2, cp * page * N_KV_HEADS, HEAD_DIM), F32),
                        pltpu.SemaphoreType.DMA((2,)),
                        pltpu.SemaphoreType.DMA((2,))],
    )
    return pl.pallas_call(
        body, grid_spec=grid_spec,
        out_shape=jax.ShapeDtypeStruct((nb, N_KV_HEADS, rows, HEAD_DIM), BF16),
        compiler_params=_params(1), name="sample_attention",
    )(page_table.reshape(-1), bias, q, kn, vn, ck, cv)


def _tail_body(x_ref, oa_ref, ol_ref, ga_ref, gb_ref, wua_ref, wul_ref, wo_ref, w1_ref, w2_ref,
               gpost_ref, gfpre_ref, gfpost_ref, y_ref):
    merged = (jax.nn.sigmoid(ga_ref[...]) * _dot(oa_ref[...], wua_ref[...])
              + jax.nn.sigmoid(gb_ref[...]) * _dot(ol_ref[...], wul_ref[...]))
    x = x_ref[...] + _rmsnorm(_dot(merged.astype(BF16), wo_ref[...]), gpost_ref[...])
    hf = _rmsnorm(x, gfpre_ref[...]).astype(BF16)
    f = jnp.square(jnp.maximum(_dot(hf, w1_ref[...]), 0.0)).astype(BF16)
    y_ref[...] = x + _rmsnorm(_dot(f, w2_ref[...]), gfpost_ref[...])


def _tail(x, oa, ol, ga, gb, wua, wul, wo, w1, w2, gpost, gfpre, gfpost, *, tm):
    n, d = x.shape
    row = lambda a: pl.BlockSpec((tm, a.shape[1]), lambda i: (i, 0))
    consts = (wua, wul, wo, w1, w2, gpost, gfpre, gfpost)
    return pl.pallas_call(
        _tail_body, grid=(n // tm,),
        in_specs=[row(x), row(oa), row(ol), row(ga), row(gb)] + [_const_spec(c.shape) for c in consts],
        out_specs=row(x), out_shape=jax.ShapeDtypeStruct((n, d), F32),
        compiler_params=_params(1), name="merge_ffn_tail",
    )(x, oa, ol, ga, gb, *consts)


def _row_tile(n, want):
    t = min(want, n)
    while n % t:
        t //= 2
    return t


def _head_pairs_t(a, n_blk, tq):
    heads, _, d = a.shape
    a = a.reshape(heads // 2, 2, n_blk, tq, d).transpose(2, 0, 4, 1, 3)
    return a.reshape(n_blk, heads // 2, d, 2 * tq)


def _layer(xp, xs, ck, cv, cik, page_base, h0, conv0, page_table, w):
    bp, sp, d_model = xp.shape
    nb, t_new, _ = xs.shape
    assert bp == 1, "prompt group is a single sequence"
    assert t_new == SUBLANES, "one sample sequence must fill one 8-row group"
    d_attn = N_HEADS * HEAD_DIM
    d_kv = N_KV_HEADS * HEAD_DIM
    d_lru = w["w_conv"].shape[1]
    dims = dict(d_attn=d_attn, d_kv=d_kv, d_lru=d_lru)

    w_in = w["w_in"]
    o_iq_end = d_attn + 2 * d_kv + IDX_HEADS * IDX_DIM
    o_iw_end = o_iq_end + IDX_HEADS
    o_ik_end = o_iw_end + IDX_DIM
    wa = w_in[:, :o_iq_end].astype(BF16)
    wi = jnp.concatenate([w_in[:, o_iw_end:o_ik_end], w_in[:, o_iq_end:o_iw_end],
                          jnp.zeros((d_model, LANES - IDX_DIM - IDX_HEADS), F32)], axis=1).astype(BF16)
    wl = w_in[:, o_ik_end:].astype(BF16)
    lru_consts = (w["w_conv"], w["b_conv"][None], w["w_rg"].astype(BF16), w["b_rg"][None],
                  w["w_ig"].astype(BF16), w["b_ig"][None], w["lru_lambda"][None])
    tail_consts = (w["w_up_attn"].astype(BF16), w["w_up_lru"].astype(BF16), w["w_out"].astype(BF16),
                   w["w_ff1"].astype(BF16), w["w_ff2"].astype(BF16),
                   w["g_mix_post"][None], w["g_ffn_pre"][None], w["g_ffn_post"][None])
    g_pre = w["g_mix_pre"][None]

    ns = nb * t_new
    xs2 = xs.reshape(ns, d_model)
    tms = _row_tile(ns, 256)
    q, k, v, kb, vb, iq, ikw, xl, xg, ga, gb = _in_projection(xs2, g_pre, wa, wi, wl, tm=tms, **dims)
    prev = jnp.pad(conv0, ((0, 0), (SUBLANES - (CONV_W - 1), 0), (0, 0))).reshape(ns, d_lru)
    h0x = jnp.broadcast_to(h0[:, None, :], (nb, t_new, d_lru)).reshape(ns, d_lru)
    o_lru, h_all = _lru_sample(xl, xg, prev, h0x, *lru_consts, tm=tms, seg=t_new)
    iq_s = iq.reshape(IDX_HEADS, nb, t_new, IDX_DIM).transpose(1, 0, 2, 3).reshape(nb, IDX_HEADS * t_new, IDX_DIM)
    iw_s = ikw[:, IDX_DIM:IDX_DIM + IDX_HEADS].reshape(nb, t_new, IDX_HEADS).transpose(0, 2, 1)
    iw_s = iw_s.reshape(nb, IDX_HEADS * t_new, 1)
    ikn = ikw[:, :IDX_DIM].reshape(nb, t_new, IDX_DIM)
    q_s = q.reshape(nb, t_new, N_KV_HEADS, GROUP, HEAD_DIM).transpose(0, 2, 3, 1, 4)
    q_s = q_s.reshape(nb, N_KV_HEADS, GROUP * t_new, HEAD_DIM)
    n_pages = page_table.shape[1]
    cp = math.gcd(n_pages, 16)
    bias = _sample_select(page_table, iq_s, iw_s, ikn, cik, cp=cp, page_base=page_base,
                          n_seq=math.gcd(nb, 16))
    o_att = _sample_attention(page_table, bias, q_s, k.reshape(nb, t_new, d_kv), v.reshape(nb, t_new, d_kv),
                              ck, cv, page=cik.shape[2], cp=cp, page_base=page_base)
    o_att = o_att.reshape(nb, N_KV_HEADS, GROUP, t_new, HEAD_DIM).transpose(0, 3, 1, 2, 4).reshape(ns, d_attn)
    ys = _tail(xs2, o_att, o_lru, ga, gb, *tail_consts, tm=tms).reshape(nb, t_new, d_model)
    st_s = (k.reshape(nb, t_new, N_KV_HEADS, HEAD_DIM), v.reshape(nb, t_new, N_KV_HEADS, HEAD_DIM), ikn,
            h_all.reshape(nb, t_new, d_lru)[:, -1], xl.reshape(nb, t_new, d_lru)[:, t_new - (CONV_W - 1):])

    xp2 = xp.reshape(sp, d_model)
    tmp = _row_tile(sp, 256)
    q, k, v, kb, vb, iq, ikw, xl, xg, ga, gb = _in_projection(xp2, g_pre, wa, wi, wl, tm=tmp, **dims)
    o_lru, h_last, tail3 = _lru_prompt(xl, xg, *lru_consts, tm=tmp)
    ik = ikw[:, :IDX_DIM]
    tq = _row_tile(sp, LANES)
    q2 = _head_pairs_t(q.reshape(sp, N_HEADS, HEAD_DIM).transpose(1, 0, 2), sp // tq, tq)
    o_att = _prompt_attention(_head_pairs_t(iq, sp // tq, tq), ikw[:, IDX_DIM:IDX_DIM + IDX_HEADS].T, q2,
                              ik.astype(BF16), kb, vb.T, tq=tq, kc=_row_tile(sp // 2, 1024))
    yp = _tail(xp2, o_att, o_lru, ga, gb, *tail_consts, tm=tmp).reshape(1, sp, d_model)
    st_p = (k.reshape(1, sp, N_KV_HEADS, HEAD_DIM), v.reshape(1, sp, N_KV_HEADS, HEAD_DIM),
            ik.reshape(1, sp, IDX_DIM), h_last[SUBLANES - 1:], tail3[None, SUBLANES - (CONV_W - 1):])
    return yp, ys, st_p, st_s


def kernel(x_prompt, x_sample, cache_k, cache_v, cache_idx_k, state_h, state_conv, page_table, w_in, w_conv, b_conv, w_rg, b_rg, w_ig, b_ig, lru_lambda, w_up_attn, w_up_lru, w_out, g_mix_pre, g_mix_post, g_ffn_pre, g_ffn_post, w_ff1, w_ff2):
    names = ("w_in", "w_conv", "b_conv", "w_rg", "b_rg", "w_ig", "b_ig", "lru_lambda", "w_up_attn",
             "w_up_lru", "w_out", "g_mix_pre", "g_mix_post", "g_ffn_pre", "g_ffn_post", "w_ff1", "w_ff2")
    stacked = (w_in, w_conv, b_conv, w_rg, b_rg, w_ig, b_ig, lru_lambda, w_up_attn, w_up_lru, w_out,
               g_mix_pre, g_mix_post, g_ffn_pre, g_ffn_post, w_ff1, w_ff2)
    depth, n_pool, page = cache_idx_k.shape[:3]
    ck = cache_k.reshape(depth * n_pool * page * N_KV_HEADS, HEAD_DIM)
    cv = cache_v.reshape(depth * n_pool * page * N_KV_HEADS, HEAD_DIM)
    cik = jnp.swapaxes(cache_idx_k, 2, 3).reshape(depth * n_pool, IDX_DIM, page)
    yp, ys = x_prompt, x_sample
    new_p, new_s = [], []
    for layer in range(depth):
        w = {n: a[layer] for n, a in zip(names, stacked)}
        yp, ys, st_p, st_s = _layer(yp, ys, ck, cv, cik, layer * n_pool, state_h[layer], state_conv[layer],
                                    page_table, w)
        new_p.append(st_p)
        new_s.append(st_s)
    stack = lambda states, j: jnp.stack([s[j] for s in states], axis=0)
    return (yp, ys) + tuple(stack(new_p, j) for j in range(5)) + tuple(stack(new_s, j) for j in range(5))
```

```python
import functools
import math

import jax
import jax.numpy as jnp
from jax import lax
from jax.experimental import pallas as pl
from jax.experimental.pallas import tpu as pltpu

N_HEADS = 8
HEAD_DIM = 128
N_KV_HEADS = 2
GROUP = N_HEADS // N_KV_HEADS
IDX_HEADS = 8
IDX_DIM = 64
TOPK_MAX = 256
LRU_BLOCKS = 8
CONV_W = 4
LRU_C = 8.0
EPS = 1e-6

LANES = 128
SUBLANES = 8
VMEM_LIMIT = 56 * 1024 * 1024

F32 = jnp.float32
BF16 = jnp.bfloat16
NEG_BIG = -1e30
IDX_BIG = 1e9
LOG2E = math.log2(math.e)


def _const_spec(shape):
    nd = len(shape)
    return pl.BlockSpec(shape, lambda *_: (0,) * nd, pipeline_mode=pl.Buffered(1))


def _params(n_grid):
    return pltpu.CompilerParams(dimension_semantics=("arbitrary",) * n_grid,
                                vmem_limit_bytes=VMEM_LIMIT)


def _rmsnorm(x, g):
    return x * lax.rsqrt(jnp.mean(x * x, axis=-1, keepdims=True) + EPS) * g


def _dot(a, b):
    return jnp.dot(a, b, preferred_element_type=F32)


def _dot_nt(a, b):
    return lax.dot_general(a, b, (((1,), (1,)), ((), ())), preferred_element_type=F32)


def _inproj_body(x_ref, g_ref, wa_ref, wi_ref, wl_ref,
                 q_ref, k_ref, v_ref, kb_ref, vb_ref, iq_ref, ikw_ref, xl_ref, xg_ref, ga_ref, gb_ref,
                 *, d_attn, d_kv, d_lru, d_model):
    hn = _rmsnorm(x_ref[...], g_ref[...]).astype(BF16)
    q_ref[...] = (_dot(hn, wa_ref[:, :d_attn]) * (HEAD_DIM ** -0.5 * LOG2E)).astype(BF16)
    k = _dot(hn, wa_ref[:, d_attn:d_attn + d_kv])
    k_ref[...] = k
    kb_ref[...] = k.astype(BF16)
    v = _dot(hn, wa_ref[:, d_attn + d_kv:d_attn + 2 * d_kv])
    v_ref[...] = v
    vb_ref[...] = v.astype(BF16)
    iq = _dot(hn, wa_ref[:, d_attn + 2 * d_kv:])
    for h in range(IDX_HEADS):
        iq_ref[h] = iq[:, h * IDX_DIM:(h + 1) * IDX_DIM].astype(BF16)
    ikw_ref[...] = _dot(hn, wi_ref[...])
    xl_ref[...] = _dot(hn, wl_ref[:, :d_lru])
    xg_ref[...] = _dot(hn, wl_ref[:, d_lru:2 * d_lru])
    ga_ref[...] = _dot(hn, wl_ref[:, 2 * d_lru:2 * d_lru + d_model])
    gb_ref[...] = _dot(hn, wl_ref[:, 2 * d_lru + d_model:])


def _in_projection(x, g, wa, wi, wl, *, d_attn, d_kv, d_lru, tm):
    n, d_model = x.shape
    grid = (n // tm,)
    row = lambda w: pl.BlockSpec((tm, w), lambda i: (i, 0))
    out_shape = (
        jax.ShapeDtypeStruct((n, d_attn), BF16),
        jax.ShapeDtypeStruct((n, d_kv), F32),
        jax.ShapeDtypeStruct((n, d_kv), F32),
        jax.ShapeDtypeStruct((n, d_kv), BF16),
        jax.ShapeDtypeStruct((n, d_kv), BF16),
        jax.ShapeDtypeStruct((IDX_HEADS, n, IDX_DIM), BF16),
        jax.ShapeDtypeStruct((n, LANES), F32),
        jax.ShapeDtypeStruct((n, d_lru), F32),
        jax.ShapeDtypeStruct((n, d_lru), F32),
        jax.ShapeDtypeStruct((n, d_model), F32),
        jax.ShapeDtypeStruct((n, d_model), F32),
    )
    out_specs = (row(d_attn), row(d_kv), row(d_kv), row(d_kv), row(d_kv),
                 pl.BlockSpec((IDX_HEADS, tm, IDX_DIM), lambda i: (0, i, 0)),
                 row(LANES), row(d_lru), row(d_lru), row(d_model), row(d_model))
    body = functools.partial(_inproj_body, d_attn=d_attn, d_kv=d_kv, d_lru=d_lru, d_model=d_model)
    return pl.pallas_call(
        body, grid=grid,
        in_specs=[row(d_model), _const_spec(g.shape), _const_spec(wa.shape), _const_spec(wi.shape),
                  _const_spec(wl.shape)],
        out_specs=out_specs, out_shape=out_shape, compiler_params=_params(1), name="in_projection",
    )(x, g, wa, wi, wl)


def _shift_rows(cur, prev, j, row_in_seg):
    tm = cur.shape[0]
    from_prev = pltpu.roll(prev, tm - SUBLANES + j, axis=0)
    return jnp.where(row_in_seg < j, from_prev, pltpu.roll(cur, j, axis=0))


def _lru_core(xl, xg, prev, h_in, wc, bc, wrg, brg, wig, big, lam, *, seg):
    tm, d = xl.shape
    row = lax.broadcasted_iota(jnp.int32, (tm, 1), 0)
    row_in_seg = row % seg
    u = bc + wc[CONV_W - 1:CONV_W] * xl
    for j in range(1, CONV_W):
        u = u + wc[CONV_W - 1 - j:CONV_W - j] * _shift_rows(xl, prev, j, row_in_seg)
    ub = u.astype(BF16)
    blk = d // LRU_BLOCKS
    r = jnp.concatenate([_dot(ub[:, n * blk:(n + 1) * blk], wrg[n]) for n in range(LRU_BLOCKS)], axis=1)
    ig = jnp.concatenate([_dot(ub[:, n * blk:(n + 1) * blk], wig[n]) for n in range(LRU_BLOCKS)], axis=1)
    r = jax.nn.sigmoid(r + brg)
    ig = jax.nn.sigmoid(ig + big)
    softplus_neg = jnp.maximum(-lam, 0.0) + jnp.log1p(jnp.exp(-jnp.abs(lam)))
    log_a = -LRU_C * r * softplus_neg
    a = jnp.exp(log_a)
    b = jnp.sqrt(-jnp.tanh(log_a) * (a * a + 1.0)) * (ig * u)
    d_step = 1
    while d_step < seg:
        keep = row_in_seg >= d_step
        a_sh = jnp.where(keep, pltpu.roll(a, d_step, axis=0), 1.0)
        b_sh = jnp.where(keep, pltpu.roll(b, d_step, axis=0), 0.0)
        b = a * b_sh + b
        a = a * a_sh
        d_step *= 2
    h = a * h_in + b
    gelu = 0.5 * xg * (1.0 + jnp.tanh(math.sqrt(2.0 / math.pi) * (xg + 0.044715 * (xg * xg * xg))))
    return h * gelu, h


def _lru_prompt_body(xl_ref, xg_ref, wc_ref, bc_ref, wrg_ref, brg_ref, wig_ref, big_ref, lam_ref,
                     o_ref, h_ref, tail_ref, hc_sc, prev_sc):
    @pl.when(pl.program_id(0) == 0)
    def _():
        hc_sc[...] = jnp.zeros_like(hc_sc)
        prev_sc[...] = jnp.zeros_like(prev_sc)

    xl = xl_ref[...]
    tm = xl.shape[0]
    prev = jnp.tile(prev_sc[...], (tm // SUBLANES, 1))
    out, h = _lru_core(xl, xg_ref[...], prev, hc_sc[0:1, :], wc_ref[...], bc_ref[...], wrg_ref[...],
                       brg_ref[...], wig_ref[...], big_ref[...], lam_ref[...], seg=tm)
    o_ref[...] = out.astype(BF16)
    last = h[tm - SUBLANES:, :]
    hc_sc[...] = jnp.broadcast_to(last[SUBLANES - 1:SUBLANES, :], hc_sc.shape)
    h_ref[...] = last
    prev_sc[...] = xl[tm - SUBLANES:, :]
    tail_ref[...] = xl[tm - SUBLANES:, :]


def _lru_prompt(xl, xg, wc, bc, wrg, brg, wig, big, lam, *, tm):
    n, d = xl.shape
    row = pl.BlockSpec((tm, d), lambda i: (i, 0))
    last8 = pl.BlockSpec((SUBLANES, d), lambda i: (0, 0))
    consts = (wc, bc, wrg, brg, wig, big, lam)
    return pl.pallas_call(
        _lru_prompt_body, grid=(n // tm,),
        in_specs=[row, row] + [_const_spec(c.shape) for c in consts],
        out_specs=(row, last8, last8),
        out_shape=(jax.ShapeDtypeStruct((n, d), BF16),
                   jax.ShapeDtypeStruct((SUBLANES, d), F32),
                   jax.ShapeDtypeStruct((SUBLANES, d), F32)),
        scratch_shapes=[pltpu.VMEM((SUBLANES, d), F32), pltpu.VMEM((SUBLANES, d), F32)],
        compiler_params=_params(1), name="lru_prompt",
    )(xl, xg, *consts)


def _lru_sample_body(xl_ref, xg_ref, prev_ref, h0_ref, wc_ref, bc_ref, wrg_ref, brg_ref, wig_ref,
                     big_ref, lam_ref, o_ref, h_ref, *, seg):
    out, h = _lru_core(xl_ref[...], xg_ref[...], prev_ref[...], h0_ref[...], wc_ref[...], bc_ref[...],
                       wrg_ref[...], brg_ref[...], wig_ref[...], big_ref[...], lam_ref[...], seg=seg)
    o_ref[...] = out.astype(BF16)
    h_ref[...] = h


def _lru_sample(xl, xg, prev, h0, wc, bc, wrg, brg, wig, big, lam, *, tm, seg):
    n, d = xl.shape
    row = pl.BlockSpec((tm, d), lambda i: (i, 0))
    consts = (wc, bc, wrg, brg, wig, big, lam)
    return pl.pallas_call(
        functools.partial(_lru_sample_body, seg=seg), grid=(n // tm,),
        in_specs=[row, row, row, row] + [_const_spec(c.shape) for c in consts],
        out_specs=(row, row),
        out_shape=(jax.ShapeDtypeStruct((n, d), BF16), jax.ShapeDtypeStruct((n, d), F32)),
        compiler_params=_params(1), name="lru_sample",
    )(xl, xg, prev, h0, *consts)


def _ordered_bits_to_float(u):
    key = u ^ jnp.int32(-2 ** 31)
    bits = jnp.where(key >= 0, key, key ^ jnp.int32(0x7FFFFFFF))
    return lax.bitcast_convert_type(bits, F32)


REFINE_STEPS = 12


def _float_to_ordered_bits(x):
    bits = lax.bitcast_convert_type(x, jnp.int32)
    key = jnp.where(bits >= 0, bits, bits ^ jnp.int32(0x7FFFFFFF))
    return key ^ jnp.int32(-2 ** 31)


def _kth_search(count_ge, k_eff, bracket=None):
    shape = k_eff.shape
    zero = jnp.zeros(shape, jnp.int32)
    if bracket is None:
        i0, u0 = jnp.int32(0), zero
    else:
        u_lo, u_hi = (_float_to_ordered_bits(x) for x in bracket)
        i0 = jnp.minimum(jnp.min(lax.clz(u_lo ^ u_hi)), 31)
        keep = lax.shift_left(lax.shift_left(jnp.int32(-1), 31 - i0), 1)
        u0 = u_hi & keep

    def cond(st):
        return (st[0] < 32) & (st[4] == 0)

    def body(st):
        i, u, thr, exact, _ = st
        cand = u | lax.shift_left(jnp.int32(1), 31 - i)
        t = _ordered_bits_to_float(cand)
        c = count_ge(t)
        hit = (c == k_eff) & (exact == 0)
        thr = jnp.where(hit, t, thr)
        exact = jnp.where(hit, 1, exact)
        u = jnp.where(c >= k_eff, cand, u)
        return i + 1, u, thr, exact, jnp.min(exact)

    init = (i0, u0, jnp.zeros(shape, F32), zero, jnp.int32(0))
    _, u, thr, exact, all_exact = lax.while_loop(cond, body, init)

    def refine_cond(st):
        return (st[0] < REFINE_STEPS) & (st[5] == 0)

    def refine(st):
        j, lo, hi, thr, exact, _ = st
        mid = lo + (hi - lo) * 0.5
        c = count_ge(mid)
        hit = (c == k_eff) & (exact == 0)
        thr = jnp.where(hit, mid, thr)
        exact = jnp.where(hit, 1, exact)
        return j + 1, jnp.where(c > k_eff, mid, lo), jnp.where(c < k_eff, mid, hi), thr, exact, jnp.min(exact)

    init = (jnp.int32(0), _ordered_bits_to_float(u), _ordered_bits_to_float(u + 1), thr, exact, all_exact)
    _, lo, _, thr, exact, _ = lax.while_loop(refine_cond, refine, init)
    return jnp.where(exact == 1, thr, lo), exact


BRACKET_BITS = 24
BRACKET_BLIND_STEPS = 14


def _bracket_search(count_ge, k_eff, bracket):
    lo, hi = bracket
    lo = jnp.maximum(lo, NEG_BIG)
    width = hi - lo
    shape = k_eff.shape
    zero = jnp.zeros(shape, jnp.int32)

    def cond(st):
        return (st[0] < BRACKET_BITS) & (st[4] == 0)

    def halve(i, f, thr, exact):
        cand = f | lax.shift_left(jnp.int32(1), BRACKET_BITS - 1 - i)
        t = lo + (cand.astype(F32) * 2.0 ** -BRACKET_BITS) * width
        c = count_ge(t)
        hit = (c == k_eff) & (exact == 0)
        return jnp.where(c >= k_eff, cand, f), jnp.where(hit, t, thr), jnp.where(hit, 1, exact)

    def body(st):
        i, f, thr, exact, _ = st
        f, thr, exact = halve(i, f, thr, exact)
        return i + 1, f, thr, exact, jnp.min(exact)

    f, thr, exact = lax.fori_loop(0, BRACKET_BLIND_STEPS, lambda i, st: halve(i, *st),
                                  (zero, jnp.zeros(shape, F32), zero))
    init = (jnp.int32(BRACKET_BLIND_STEPS), f, thr, exact, jnp.min(exact))
    _, f, thr, exact, _ = lax.while_loop(cond, body, init)
    top = f == 2 ** BRACKET_BITS - 1
    t_above = jnp.where(top, jnp.inf, lo + ((f + 1).astype(F32) * 2.0 ** -BRACKET_BITS) * width)
    return thr, exact, t_above


def _tie_cut(count_lt, k_eff, n_bits):
    def step(i, j):
        cand = j | lax.shift_left(jnp.int32(1), n_bits - 1 - i)
        ok = count_lt(cand.astype(F32)) < k_eff
        return jnp.where(ok, cand, j)
    j = lax.fori_loop(0, n_bits, step, jnp.zeros(k_eff.shape, jnp.int32))
    return j.astype(F32)


def _rank(score, thr, key_idx):
    return jnp.where(score > thr, -1.0, jnp.where(score == thr, key_idx, IDX_BIG))


def _tree_sum(terms):
    while len(terms) > 1:
        terms = [a + b for a, b in zip(terms[0::2], terms[1::2])] + ([terms[-1]] if len(terms) % 2 else [])
    return terms[0]


COUNT_ROWS = 64


def _pattn_body(iq2_ref, iwt_ref, q2_ref, ik_ref, k_ref, vt_ref, o_ref,
                s_sc, cm_sc, sa_sc, sb_sc, pa_sc, pb_sc, thr_sc, m_sc, l_sc, acc_sc,
                *, tq, kc, n_sel, n_bits):
    qb = pl.program_id(0)
    n_chunks = ((qb + 1) * tq + kc - 1) // kc
    n_chunks_even = (n_chunks + 1) // 2 * 2
    q_pos = qb * tq + lax.broadcasted_iota(jnp.int32, (1, tq), 1)
    n_pairs = N_HEADS // 2
    iw = iwt_ref[...] * (IDX_DIM ** -0.5 * IDX_HEADS ** -0.5)

    def chunk_off(c):
        return pl.multiple_of(c * kc, kc)

    def score_chunk(c, carry):
        off = chunk_off(c)
        ikc = ik_ref[pl.ds(off, kc), :]
        acc = jnp.zeros((kc, tq), F32)
        for hp in range(IDX_HEADS // 2):
            lg = _dot(ikc, iq2_ref[0, hp])
            acc = acc + jnp.maximum(lg[:, :tq], 0.0) * iw[2 * hp:2 * hp + 1, :]
            acc = acc + jnp.maximum(lg[:, tq:], 0.0) * iw[2 * hp + 1:2 * hp + 2, :]
        k_pos = off + lax.broadcasted_iota(jnp.int32, (kc, 1), 0)
        sc = jnp.where(k_pos <= q_pos, acc, -jnp.inf)
        s_sc[pl.ds(off, kc), :] = sc
        cm_sc[...] = functools.reduce(jnp.maximum, [cm_sc[...]] + [
            sc[j * TOPK_MAX:(j + 1) * TOPK_MAX] for j in range(kc // TOPK_MAX)])
        return carry
    cm_sc[...] = jnp.full_like(cm_sc, -jnp.inf)
    lax.fori_loop(0, n_chunks_even, score_chunk, 0)
    bracket = (jnp.min(cm_sc[...], axis=0, keepdims=True), jnp.max(cm_sc[...], axis=0, keepdims=True))

    def count_where(pred):
        def body(c, acc):
            off = chunk_off(c)
            for j in range(kc // COUNT_ROWS):
                acc = acc + jnp.where(pred(s_sc[pl.ds(off + j * COUNT_ROWS, COUNT_ROWS), :]), 1, 0)
            return acc
        acc = lax.fori_loop(0, n_chunks, body, jnp.zeros((COUNT_ROWS, tq), jnp.int32))
        return jnp.sum(acc, axis=0, keepdims=True)

    k_eff = jnp.minimum(q_pos + 1, n_sel)
    def max_below(t):
        def body(c, acc):
            off = chunk_off(c)
            for j in range(kc // COUNT_ROWS):
                blk = s_sc[pl.ds(off + j * COUNT_ROWS, COUNT_ROWS), :]
                acc = jnp.maximum(acc, jnp.where(blk < t, blk, -jnp.inf))
            return acc
        acc = lax.fori_loop(0, n_chunks, body, jnp.full((COUNT_ROWS, tq), -jnp.inf, F32))
        return jnp.max(acc, axis=0, keepdims=True)

    def break_ties(thr):
        def rank_chunk(c, carry):
            off = chunk_off(c)
            key_idx = (off + lax.broadcasted_iota(jnp.int32, (kc, 1), 0)).astype(F32)
            s_sc[pl.ds(off, kc), :] = _rank(s_sc[pl.ds(off, kc), :], thr, key_idx)
            return carry
        lax.fori_loop(0, n_chunks, rank_chunk, 0)
        cut = _tie_cut(lambda j: count_where(lambda blk: blk < j), k_eff, n_bits)

        def recode_chunk(c, carry):
            off = chunk_off(c)
            s_sc[pl.ds(off, kc), :] = jnp.where(s_sc[pl.ds(off, kc), :] <= cut, 1.0, -1.0)
            return carry
        lax.fori_loop(0, n_chunks, recode_chunk, 0)
        thr_sc[...] = jnp.zeros_like(thr_sc)

    count_ge = lambda t: count_where(lambda blk: blk >= t)
    thr_b, exact_b, t_above = _bracket_search(count_ge, k_eff, bracket)
    thr_sc[...] = jnp.broadcast_to(thr_b, thr_sc.shape)

    @pl.when(jnp.min(exact_b) == 0)
    def _():
        v = max_below(t_above)
        n_ge = count_ge(v)
        n_gt = count_where(lambda blk: blk > v)
        found = (exact_b == 1) | ((n_gt < k_eff) & (n_ge >= k_eff))
        thr_v = jnp.where(exact_b == 1, thr_b, v)
        tied = (exact_b == 0) & (n_ge > k_eff)
        thr_sc[...] = jnp.broadcast_to(thr_v, thr_sc.shape)
        all_found = jnp.min(found.astype(jnp.int32))

        @pl.when((all_found == 1) & (jnp.max(tied.astype(jnp.int32)) == 1))
        def _():
            break_ties(thr_v)

        @pl.when(all_found == 0)
        def _():
            thr, exact = _kth_search(count_ge, k_eff, bracket)
            thr_sc[...] = jnp.broadcast_to(thr, thr_sc.shape)

            @pl.when(jnp.min(exact) == 0)
            def _():
                break_ties(thr)

    sel_thr = thr_sc[0:1, :]
    m_sc[...] = jnp.full_like(m_sc, NEG_BIG)
    l_sc[...] = jnp.zeros_like(l_sc)
    acc_sc[...] = jnp.zeros_like(acc_sc)
    c2 = 1.0
    last_chunk = k_ref.shape[0] // kc - 1

    def qk(c, s_buf):
        off = chunk_off(jnp.minimum(c, last_chunk))
        for hp in range(n_pairs):
            g = (2 * hp) // GROUP
            s_buf[:, 2 * hp * tq:(2 * hp + 2) * tq] = _dot(
                k_ref[pl.ds(off, kc), g * HEAD_DIM:(g + 1) * HEAD_DIM], q2_ref[0, hp])

    def pv(c, p_buf, alphas):
        off = chunk_off(jnp.maximum(c, 0))
        for hp in range(n_pairs):
            g = (2 * hp) // GROUP
            acc_sc[hp] = alphas[hp] * acc_sc[hp] + _dot(
                vt_ref[g * HEAD_DIM:(g + 1) * HEAD_DIM, pl.ds(off, kc)],
                p_buf[:, 2 * hp * tq:(2 * hp + 2) * tq])

    def softmax(c, s_buf, p_buf):
        bias = jnp.where(s_sc[pl.ds(chunk_off(c), kc), :] >= sel_thr, 0.0, NEG_BIG)
        alphas = []
        for h in range(N_HEADS):
            s = s_buf[:, h * tq:(h + 1) * tq] + bias
            s_buf[:, h * tq:(h + 1) * tq] = s
            m_old = m_sc[h]
            m_new = jnp.maximum(m_old, jnp.max(s, axis=0, keepdims=True))
            alpha = jnp.exp2((m_old - m_new) * c2)
            p = jnp.exp2((s_buf[:, h * tq:(h + 1) * tq] - m_new[0:1, :]) * c2)
            l_sc[h] = alpha * l_sc[h] + jnp.sum(p, axis=0, keepdims=True)
            m_sc[h] = m_new
            p_buf[:, h * tq:(h + 1) * tq] = p.astype(BF16)
            alphas.append(alpha[0:1, :])
        return [jnp.concatenate(alphas[2 * hp:2 * hp + 2], axis=1) for hp in range(n_pairs)]

    def stage(c, s_cur, p_cur, s_next, p_prev, alphas_prev):
        qk(c + 1, s_next)
        pv(c - 1, p_prev, alphas_prev)
        return softmax(c, s_cur, p_cur)

    qk(0, sa_sc)
    pb_sc[...] = jnp.zeros_like(pb_sc)

    def attn_two_chunks(i, alphas):
        alphas = stage(2 * i, sa_sc, pa_sc, sb_sc, pb_sc, alphas)
        return stage(2 * i + 1, sb_sc, pb_sc, sa_sc, pa_sc, alphas)
    ones = [jnp.ones((1, 2 * tq), F32)] * n_pairs
    alphas = lax.fori_loop(0, n_chunks_even // 2, attn_two_chunks, ones)
    pv(n_chunks_even - 1, pb_sc, alphas)

    for h in range(N_HEADS):
        o_t = acc_sc[h // 2][:, (h % 2) * tq:(h % 2 + 1) * tq] / l_sc[h][0:1, :]
        o_ref[:, h * HEAD_DIM:(h + 1) * HEAD_DIM] = o_t.T.astype(BF16)


def _prompt_attention(iq2, iwt, q2, ik, kb, vt, *, tq, kc):
    n = kb.shape[0]
    d_attn = N_HEADS * HEAD_DIM
    n_sel = min(TOPK_MAX, n // 4)
    n_bits = max(1, (n - 1).bit_length())
    assert (n // kc) % 2 == 0, "the attention pipeline consumes key chunks in pairs"
    assert kc % TOPK_MAX == 0, "a key chunk holds whole sets of residue classes"
    body = functools.partial(_pattn_body, tq=tq, kc=kc, n_sel=n_sel, n_bits=n_bits)
    blk4 = lambda a: pl.BlockSpec((1,) + a.shape[1:], lambda i: (i, 0, 0, 0))
    return pl.pallas_call(
        body, grid=(n // tq,),
        in_specs=[blk4(iq2), pl.BlockSpec((IDX_HEADS, tq), lambda i: (0, i)), blk4(q2),
                  _const_spec(ik.shape), _const_spec(kb.shape), _const_spec(vt.shape)],
        out_specs=pl.BlockSpec((tq, d_attn), lambda i: (i, 0)),
        out_shape=jax.ShapeDtypeStruct((n, d_attn), BF16),
        scratch_shapes=[pltpu.VMEM((n, tq), F32),
                        pltpu.VMEM((TOPK_MAX, tq), F32),
                        pltpu.VMEM((kc, N_HEADS * tq), F32),
                        pltpu.VMEM((kc, N_HEADS * tq), F32),
                        pltpu.VMEM((kc, N_HEADS * tq), BF16),
                        pltpu.VMEM((kc, N_HEADS * tq), BF16),
                        pltpu.VMEM((SUBLANES, tq), F32),
                        pltpu.VMEM((N_HEADS, SUBLANES, tq), F32),
                        pltpu.VMEM((N_HEADS, SUBLANES, tq), F32),
                        pltpu.VMEM((N_HEADS // 2, HEAD_DIM, 2 * tq), F32)],
        compiler_params=_params(1), name="prompt_attention",
    )(iq2, iwt, q2, ik, kb, vt)


def _ssel_body(pt_ref, iq_ref, iw_ref, ikn_ref, cik_hbm, o_ref,
               ik_buf, s_sc, snew_sc, cm_sc, thr_sc, ik_sem,
               *, n_pages, page, cp, t_new, n_sel, n_bits, page_base, n_seq):
    grp = pl.program_id(0)
    nb = pl.num_programs(0) * n_seq
    past = n_pages * page
    n_ch = n_pages // cp
    kc = cp * page
    rows = n_seq * t_new

    def start_ik(seq, slot):
        def body(p, c):
            pltpu.make_async_copy(cik_hbm.at[pt_ref[seq * n_pages + p] + page_base],
                                  ik_buf.at[slot, :, pl.ds(p * page, page)], ik_sem.at[slot]).start()
            return c
        lax.fori_loop(0, n_pages, body, 0)

    def wait_ik(slot):
        pltpu.make_async_copy(ik_buf.at[slot], ik_buf.at[slot], ik_sem.at[slot]).wait()

    @pl.when(grp == 0)
    def _():
        start_ik(0, 0)

    t_idx = lax.broadcasted_iota(jnp.int32, (t_new, LANES), 0)
    j_idx = lax.broadcasted_iota(jnp.int32, (t_new, LANES), 1)

    def head_sum(x):
        return _tree_sum([x[h * t_new:(h + 1) * t_new] for h in range(IDX_HEADS)])

    def score_seq(i, carry):
        seq = grp * n_seq + i
        slot = seq % 2

        @pl.when(seq + 1 < nb)
        def _():
            start_ik(seq + 1, 1 - slot)

        wait_ik(slot)
        iq = iq_ref[i]
        iw = iw_ref[i] * (IDX_DIM ** -0.5 * IDX_HEADS ** -0.5)
        r0 = pl.multiple_of(i * t_new, t_new)
        cm = [jnp.full((t_new, LANES), -jnp.inf, F32)] * 2
        for c in range(n_ch):
            ikc = ik_buf[slot, :, c * kc:(c + 1) * kc].astype(BF16)
            sc = head_sum(jnp.maximum(_dot(iq, ikc), 0.0) * iw)
            s_sc[pl.ds(r0, t_new), c * kc:(c + 1) * kc] = sc
            for j in range(kc // LANES):
                cm[j % 2] = jnp.maximum(cm[j % 2], sc[:, j * LANES:(j + 1) * LANES])
        cm_sc[pl.ds(r0, t_new), :] = jnp.concatenate(cm, axis=1)
        ikn = jnp.concatenate([ikn_ref[i], jnp.zeros((LANES - t_new, IDX_DIM), F32)], axis=0)
        s_new = head_sum(jnp.maximum(_dot_nt(iq, ikn.astype(BF16)), 0.0) * iw)
        snew_sc[pl.ds(r0, t_new), :] = jnp.where(j_idx <= t_idx, s_new, -jnp.inf)
        return carry
    lax.fori_loop(0, n_seq, score_seq, 0)

    def count_where(pred):
        terms = [jnp.where(pred(snew_sc[...]), 1, 0)]
        for j in range(past // LANES):
            terms.append(jnp.where(pred(s_sc[:, j * LANES:(j + 1) * LANES]), 1, 0))
        return jnp.sum(_tree_sum(terms), axis=1, keepdims=True)

    k_eff = jnp.full((rows, 1), n_sel, jnp.int32)
    assert n_sel <= 2 * LANES <= past
    lo = jnp.min(cm_sc[...], axis=1, keepdims=True)
    hi = jnp.maximum(jnp.max(cm_sc[...], axis=1, keepdims=True), jnp.max(snew_sc[...], axis=1, keepdims=True))
    count_ge = lambda t: count_where(lambda blk: blk >= t)
    thr_b, exact_b, _ = _bracket_search(count_ge, k_eff, (lo, hi))
    thr_sc[...] = jnp.broadcast_to(thr_b, thr_sc.shape)

    @pl.when(jnp.min(exact_b) == 0)
    def _():
        thr, exact = _kth_search(count_ge, k_eff, (lo, hi))
        thr_sc[...] = jnp.broadcast_to(thr, thr_sc.shape)

        @pl.when(jnp.min(exact) == 0)
        def _():
            lane_f = lax.broadcasted_iota(jnp.int32, (1, kc), 1)
            j_all = lax.broadcasted_iota(jnp.int32, (1, LANES), 1)
            for c in range(n_ch):
                s_sc[:, c * kc:(c + 1) * kc] = _rank(s_sc[:, c * kc:(c + 1) * kc], thr,
                                                     (c * kc + lane_f).astype(F32))
            snew_sc[...] = _rank(snew_sc[...], thr, (past + j_all).astype(F32))
            cut = _tie_cut(lambda j: count_where(lambda blk: blk < j), k_eff, n_bits)
            for c in range(n_ch):
                s_sc[:, c * kc:(c + 1) * kc] = jnp.where(s_sc[:, c * kc:(c + 1) * kc] <= cut, 1.0, -1.0)
            snew_sc[...] = jnp.where(snew_sc[...] <= cut, 1.0, -1.0)
            thr_sc[...] = jnp.zeros_like(thr_sc)

    sel_thr = thr_sc[:, 0:1]
    for c in range(n_ch):
        o_ref[:, c * kc:(c + 1) * kc] = jnp.where(s_sc[:, c * kc:(c + 1) * kc] >= sel_thr, 0.0, NEG_BIG)
    o_ref[:, past:] = jnp.where(snew_sc[...] >= sel_thr, 0.0, NEG_BIG)


def _sample_select(page_table, iq, iw, ikn, cik, *, cp, page_base, n_seq):
    nb, n_pages = page_table.shape
    page = cik.shape[2]
    t_new = ikn.shape[1]
    past = n_pages * page
    n_sel = min(TOPK_MAX, (past + t_new) // 4)
    n_bits = max(1, (past + t_new - 1).bit_length())
    rows = n_seq * t_new
    body = functools.partial(_ssel_body, n_pages=n_pages, page=page, cp=cp, t_new=t_new,
                             n_sel=n_sel, n_bits=n_bits, page_base=page_base, n_seq=n_seq)
    grp3 = lambda a: pl.BlockSpec((n_seq,) + a.shape[1:], lambda i, pt: (i, 0, 0))
    grid_spec = pltpu.PrefetchScalarGridSpec(
        num_scalar_prefetch=1, grid=(nb // n_seq,),
        in_specs=[grp3(iq), grp3(iw), grp3(ikn), pl.BlockSpec(memory_space=pl.ANY)],
        out_specs=pl.BlockSpec((rows, past + LANES), lambda i, pt: (i, 0)),
        scratch_shapes=[pltpu.VMEM((2, IDX_DIM, past), F32),
                        pltpu.VMEM((rows, past), F32),
                        pltpu.VMEM((rows, LANES), F32),
                        pltpu.VMEM((rows, 2 * LANES), F32),
                        pltpu.VMEM((rows, LANES), F32),
                        pltpu.SemaphoreType.DMA((2,))],
    )
    return pl.pallas_call(
        body, grid_spec=grid_spec,
        out_shape=jax.ShapeDtypeStruct((nb * t_new, past + LANES), F32),
        compiler_params=_params(1), name="sample_select",
    )(page_table.reshape(-1), iq, iw, ikn, cik)


def _sattn_body(pt_ref, bias_ref, q_ref, kn_ref, vn_ref, ck_hbm, cv_hbm,
                o_ref, k_buf, v_buf, k_sem, v_sem, *, n_pages, page, cp, t_new, page_base):
    b = pl.program_id(0)
    nb = pl.num_programs(0)
    past = n_pages * page
    n_ch = n_pages // cp
    kc = cp * page
    page_rows = page * N_KV_HEADS

    def start_kv(seq, chunk, slot):
        def body(p, c):
            src = pl.ds((pt_ref[seq * n_pages + chunk * cp + p] + page_base) * page_rows, page_rows)
            dst = pl.ds(p * page_rows, page_rows)
            pltpu.make_async_copy(ck_hbm.at[src, :], k_buf.at[slot, dst, :], k_sem.at[slot]).start()
            pltpu.make_async_copy(cv_hbm.at[src, :], v_buf.at[slot, dst, :], v_sem.at[slot]).start()
            return c
        lax.fori_loop(0, cp, body, 0)

    def wait_kv(slot):
        pltpu.make_async_copy(k_buf.at[slot], k_buf.at[slot], k_sem.at[slot]).wait()
        pltpu.make_async_copy(v_buf.at[slot], v_buf.at[slot], v_sem.at[slot]).wait()

    @pl.when(b == 0)
    def _():
        start_kv(0, 0, 0)

    def pad_keys(x):
        return jnp.concatenate([x, jnp.zeros((LANES - t_new, x.shape[1]), F32)], axis=0).astype(BF16)

    c2 = 1.0
    bias_new = jnp.tile(bias_ref[:, past:], (GROUP, 1))
    m = []
    l = []
    acc = []
    for g in range(N_KV_HEADS):
        kn = pad_keys(kn_ref[0, :, g * HEAD_DIM:(g + 1) * HEAD_DIM])
        vn = pad_keys(vn_ref[0, :, g * HEAD_DIM:(g + 1) * HEAD_DIM])
        s = _dot_nt(q_ref[0, g], kn) + bias_new
        m_g = jnp.max(s, axis=1, keepdims=True)
        p = jnp.exp2((s - m_g) * c2)
        m.append(m_g)
        l.append(jnp.sum(p, axis=1, keepdims=True))
        acc.append(_dot(p.astype(BF16), vn))

    for c in range(n_ch):
        slot = (b * n_ch + c) % 2
        if c + 1 < n_ch:
            start_kv(b, c + 1, 1 - slot)
        else:
            @pl.when(b + 1 < nb)
            def _():
                start_kv(b + 1, 0, 1 - slot)
        wait_kv(slot)
        bias = jnp.tile(bias_ref[:, c * kc:(c + 1) * kc], (GROUP, 1))
        for g in range(N_KV_HEADS):
            kg = k_buf[slot, pl.ds(g, kc, stride=N_KV_HEADS), :].astype(BF16)
            vg = v_buf[slot, pl.ds(g, kc, stride=N_KV_HEADS), :].astype(BF16)
            s = _dot_nt(q_ref[0, g], kg) + bias
            m_new = jnp.maximum(m[g], jnp.max(s, axis=1, keepdims=True))
            alpha = jnp.exp2((m[g] - m_new) * c2)
            p = jnp.exp2((s - m_new) * c2)
            l[g] = alpha * l[g] + jnp.sum(p, axis=1, keepdims=True)
            acc[g] = alpha * acc[g] + _dot(p.astype(BF16), vg)
            m[g] = m_new

    for g in range(N_KV_HEADS):
        o_ref[0, g] = (acc[g] / l[g]).astype(BF16)


def _sample_attention(page_table, bias, q, kn, vn, ck, cv, *, page, cp, page_base):
    nb, n_pages = page_table.shape
    t_new = kn.shape[1]
    rows = GROUP * t_new
    body = functools.partial(_sattn_body, n_pages=n_pages, page=page, cp=cp, t_new=t_new,
                             page_base=page_base)
    seq3 = lambda a: pl.BlockSpec((1,) + a.shape[1:], lambda i, pt: (i, 0, 0))
    seq4 = lambda a: pl.BlockSpec((1,) + a.shape[1:], lambda i, pt: (i, 0, 0, 0))
    any_spec = pl.BlockSpec(memory_space=pl.ANY)
    grid_spec = pltpu.PrefetchScalarGridSpec(
        num_scalar_prefetch=1, grid=(nb,),
        in_specs=[pl.BlockSpec((t_new, bias.shape[1]), lambda i, pt: (i, 0)), seq4(q), seq3(kn), seq3(vn),
                  any_spec, any_spec],
        out_specs=pl.BlockSpec((1, N_KV_HEADS, rows, HEAD_DIM), lambda i, pt: (i, 0, 0, 0)),
        scratch_shapes=[pltpu.VMEM((2, cp * page * N_KV_HEADS, HEAD_DIM), F32),
                        pltpu.VMEM((2, cp * page * N_KV_HEADS, HEAD_DIM), F32),
                        pltpu.SemaphoreType.DMA((2,)),
                        pltpu.SemaphoreType.DMA((2,))],
    )
    return pl.pallas_call(
        body, grid_spec=grid_spec,
        out_shape=jax.ShapeDtypeStruct((nb, N_KV_HEADS, rows, HEAD_DIM), BF16),
        compiler_params=_params(1), name="sample_attention",
    )(page_table.reshape(-1), bias, q, kn, vn, ck, cv)


def _tail_body(x_ref, oa_ref, ol_ref, ga_ref, gb_ref, wua_ref, wul_ref, wo_ref, w1_ref, w2_ref,
               gpost_ref, gfpre_ref, gfpost_ref, y_ref):
    merged = (jax.nn.sigmoid(ga_ref[...]) * _dot(oa_ref[...], wua_ref[...])
              + jax.nn.sigmoid(gb_ref[...]) * _dot(ol_ref[...], wul_ref[...]))
    x = x_ref[...] + _rmsnorm(_dot(merged.astype(BF16), wo_ref[...]), gpost_ref[...])
    hf = _rmsnorm(x, gfpre_ref[...]).astype(BF16)
    f = jnp.square(jnp.maximum(_dot(hf, w1_ref[...]), 0.0)).astype(BF16)
    y_ref[...] = x + _rmsnorm(_dot(f, w2_ref[...]), gfpost_ref[...])


def _tail(x, oa, ol, ga, gb, wua, wul, wo, w1, w2, gpost, gfpre, gfpost, *, tm):
    n, d = x.shape
    row = lambda a: pl.BlockSpec((tm, a.shape[1]), lambda i: (i, 0))
    consts = (wua, wul, wo, w1, w2, gpost, gfpre, gfpost)
    return pl.pallas_call(
        _tail_body, grid=(n // tm,),
        in_specs=[row(x), row(oa), row(ol), row(ga), row(gb)] + [_const_spec(c.shape) for c in consts],
        out_specs=row(x), out_shape=jax.ShapeDtypeStruct((n, d), F32),
        compiler_params=_params(1), name="merge_ffn_tail",
    )(x, oa, ol, ga, gb, *consts)


def _row_tile(n, want):
    t = min(want, n)
    while n % t:
        t //= 2
    return t


def _head_pairs_t(a, n_blk, tq):
    heads, _, d = a.shape
    a = a.reshape(heads // 2, 2, n_blk, tq, d).transpose(2, 0, 4, 1, 3)
    return a.reshape(n_blk, heads // 2, d, 2 * tq)


def _layer(xp, xs, ck, cv, cik, page_base, h0, conv0, page_table, w):
    bp, sp, d_model = xp.shape
    nb, t_new, _ = xs.shape
    assert bp == 1, "prompt group is a single sequence"
    assert t_new == SUBLANES, "one sample sequence must fill one 8-row group"
    d_attn = N_HEADS * HEAD_DIM
    d_kv = N_KV_HEADS * HEAD_DIM
    d_lru = w["w_conv"].shape[1]
    dims = dict(d_attn=d_attn, d_kv=d_kv, d_lru=d_lru)

    w_in = w["w_in"]
    o_iq_end = d_attn + 2 * d_kv + IDX_HEADS * IDX_DIM
    o_iw_end = o_iq_end + IDX_HEADS
    o_ik_end = o_iw_end + IDX_DIM
    wa = w_in[:, :o_iq_end].astype(BF16)
    wi = jnp.concatenate([w_in[:, o_iw_end:o_ik_end], w_in[:, o_iq_end:o_iw_end],
                          jnp.zeros((d_model, LANES - IDX_DIM - IDX_HEADS), F32)], axis=1).astype(BF16)
    wl = w_in[:, o_ik_end:].astype(BF16)
    lru_consts = (w["w_conv"], w["b_conv"][None], w["w_rg"].astype(BF16), w["b_rg"][None],
                  w["w_ig"].astype(BF16), w["b_ig"][None], w["lru_lambda"][None])
    tail_consts = (w["w_up_attn"].astype(BF16), w["w_up_lru"].astype(BF16), w["w_out"].astype(BF16),
                   w["w_ff1"].astype(BF16), w["w_ff2"].astype(BF16),
                   w["g_mix_post"][None], w["g_ffn_pre"][None], w["g_ffn_post"][None])
    g_pre = w["g_mix_pre"][None]

    ns = nb * t_new
    xs2 = xs.reshape(ns, d_model)
    tms = _row_tile(ns, 256)
    q, k, v, kb, vb, iq, ikw, xl, xg, ga, gb = _in_projection(xs2, g_pre, wa, wi, wl, tm=tms, **dims)
    prev = jnp.pad(conv0, ((0, 0), (SUBLANES - (CONV_W - 1), 0), (0, 0))).reshape(ns, d_lru)
    h0x = jnp.broadcast_to(h0[:, None, :], (nb, t_new, d_lru)).reshape(ns, d_lru)
    o_lru, h_all = _lru_sample(xl, xg, prev, h0x, *lru_consts, tm=tms, seg=t_new)
    iq_s = iq.reshape(IDX_HEADS, nb, t_new, IDX_DIM).transpose(1, 0, 2, 3).reshape(nb, IDX_HEADS * t_new, IDX_DIM)
    iw_s = ikw[:, IDX_DIM:IDX_DIM + IDX_HEADS].reshape(nb, t_new, IDX_HEADS).transpose(0, 2, 1)
    iw_s = iw_s.reshape(nb, IDX_HEADS * t_new, 1)
    ikn = ikw[:, :IDX_DIM].reshape(nb, t_new, IDX_DIM)
    q_s = q.reshape(nb, t_new, N_KV_HEADS, GROUP, HEAD_DIM).transpose(0, 2, 3, 1, 4)
    q_s = q_s.reshape(nb, N_KV_HEADS, GROUP * t_new, HEAD_DIM)
    n_pages = page_table.shape[1]
    cp = math.gcd(n_pages, 16)
    bias = _sample_select(page_table, iq_s, iw_s, ikn, cik, cp=cp, page_base=page_base,
                          n_seq=math.gcd(nb, 16))
    o_att = _sample_attention(page_table, bias, q_s, k.reshape(nb, t_new, d_kv), v.reshape(nb, t_new, d_kv),
                              ck, cv, page=cik.shape[2], cp=cp, page_base=page_base)
    o_att = o_att.reshape(nb, N_KV_HEADS, GROUP, t_new, HEAD_DIM).transpose(0, 3, 1, 2, 4).reshape(ns, d_attn)
    ys = _tail(xs2, o_att, o_lru, ga, gb, *tail_consts, tm=tms).reshape(nb, t_new, d_model)
    st_s = (k.reshape(nb, t_new, N_KV_HEADS, HEAD_DIM), v.reshape(nb, t_new, N_KV_HEADS, HEAD_DIM), ikn,
            h_all.reshape(nb, t_new, d_lru)[:, -1], xl.reshape(nb, t_new, d_lru)[:, t_new - (CONV_W - 1):])

    xp2 = xp.reshape(sp, d_model)
    tmp = _row_tile(sp, 256)
    q, k, v, kb, vb, iq, ikw, xl, xg, ga, gb = _in_projection(xp2, g_pre, wa, wi, wl, tm=tmp, **dims)
    o_lru, h_last, tail3 = _lru_prompt(xl, xg, *lru_consts, tm=tmp)
    ik = ikw[:, :IDX_DIM]
    tq = _row_tile(sp, LANES)
    q2 = _head_pairs_t(q.reshape(sp, N_HEADS, HEAD_DIM).transpose(1, 0, 2), sp // tq, tq)
    o_att = _prompt_attention(_head_pairs_t(iq, sp // tq, tq), ikw[:, IDX_DIM:IDX_DIM + IDX_HEADS].T, q2,
                              ik.astype(BF16), kb, vb.T, tq=tq, kc=_row_tile(sp // 2, 1024))
    yp = _tail(xp2, o_att, o_lru, ga, gb, *tail_consts, tm=tmp).reshape(1, sp, d_model)
    st_p = (k.reshape(1, sp, N_KV_HEADS, HEAD_DIM), v.reshape(1, sp, N_KV_HEADS, HEAD_DIM),
            ik.reshape(1, sp, IDX_DIM), h_last[SUBLANES - 1:], tail3[None, SUBLANES - (CONV_W - 1):])
    return yp, ys, st_p, st_s


def kernel(x_prompt, x_sample, cache_k, cache_v, cache_idx_k, state_h, state_conv, page_table, w_in, w_conv, b_conv, w_rg, b_rg, w_ig, b_ig, lru_lambda, w_up_attn, w_up_lru, w_out, g_mix_pre, g_mix_post, g_ffn_pre, g_ffn_post, w_ff1, w_ff2):
    names = ("w_in", "w_conv", "b_conv", "w_rg", "b_rg", "w_ig", "b_ig", "lru_lambda", "w_up_attn",
             "w_up_lru", "w_out", "g_mix_pre", "g_mix_post", "g_ffn_pre", "g_ffn_post", "w_ff1", "w_ff2")
    stacked = (w_in, w_conv, b_conv, w_rg, b_rg, w_ig, b_ig, lru_lambda, w_up_attn, w_up_lru, w_out,
               g_mix_pre, g_mix_post, g_ffn_pre, g_ffn_post, w_ff1, w_ff2)
    depth, n_pool, page = cache_idx_k.shape[:3]
    ck = cache_k.reshape(depth * n_pool * page * N_KV_HEADS, HEAD_DIM)
    cv = cache_v.reshape(depth * n_pool * page * N_KV_HEADS, HEAD_DIM)
    cik = jnp.swapaxes(cache_idx_k, 2, 3).reshape(depth * n_pool, IDX_DIM, page)
    yp, ys = x_prompt, x_sample
    new_p, new_s = [], []
    for layer in range(depth):
        w = {n: a[layer] for n, a in zip(names, stacked)}
        yp, ys, st_p, st_s = _layer(yp, ys, ck, cv, cik, layer * n_pool, state_h[layer], state_conv[layer],
                                    page_table, w)
        new_p.append(st_p)
        new_s.append(st_s)
    stack = lambda states, j: jnp.stack([s[j] for s in states], axis=0)
    return (yp, ys) + tuple(stack(new_p, j) for j in range(5)) + tuple(stack(new_s, j) for j in range(5))
```
